```python
import math
import jax, jax.numpy as jnp
from jax import lax
import numpy as np

D_MODEL = 1024
BATCH = 2
SEQ = 8192
DEPTH = 4

N_MIXERS = 2
N_HEADS = 16
HEAD_DIM = D_MODEL // N_HEADS
BLOCK_Q = 128
POOL_WINDOWS = (2, 4, 8, 16)
N_POOL_GROUPS = len(POOL_WINDOWS)
POOL_GROUP = D_MODEL // N_POOL_GROUPS
D_FF = 2816
PLE_DIM = 256
EPS = 1e-6

kernel_name = "hybrid_stickbreak_pool_macaron"


def rms_norm(x, g):
    xf = x.astype(jnp.float32)
    y = xf * lax.rsqrt(jnp.mean(xf * xf, axis=-1, keepdims=True) + EPS)
    return (y * g.astype(jnp.float32)).astype(x.dtype)


def swiglu(h, w_gu, w_down):
    gate, up = jnp.split(h @ w_gu, 2, axis=-1)
    return (jax.nn.silu(gate) * up) @ w_down


def stick_breaking_attention(h, w_qkv, q_gain, k_gain, w_o):
    B, S, _ = h.shape
    q, k, v = jnp.split(h @ w_qkv, 3, axis=-1)
    q = rms_norm(q.reshape(B, S, N_HEADS, HEAD_DIM), q_gain)
    k = rms_norm(k.reshape(B, S, N_HEADS, HEAD_DIM), k_gain)
    v = v.reshape(B, S, N_HEADS, HEAD_DIM)
    q, k, v = (t.transpose(0, 2, 1, 3) for t in (q, k, v))
    scale = 1.0 / math.sqrt(HEAD_DIM)
    outs = []
    for blk in range(S // BLOCK_Q):
        t0 = blk * BLOCK_Q
        t1 = t0 + BLOCK_Q
        qb = q[:, :, t0:t1]
        kb = k[:, :, :t1]
        vb = v[:, :, :t1]
        z = jnp.einsum('bhqd,bhkd->bhqk', qb, kb).astype(jnp.float32) * scale
        qpos = t0 + jnp.arange(BLOCK_Q)
        kpos = jnp.arange(t1)
        causal = kpos[None, :] < qpos[:, None]
        log_stay = jnp.where(causal, jax.nn.log_sigmoid(-z), 0.0)
        log_after = lax.cumsum(log_stay, axis=3, reverse=True) - log_stay
        weights = jnp.where(causal, jnp.exp(jax.nn.log_sigmoid(z) + log_after), 0.0)
        outs.append(jnp.einsum('bhqk,bhkd->bhqd', weights.astype(vb.dtype), vb))
    o = jnp.concatenate(outs, axis=2)
    o = o.transpose(0, 2, 1, 3).reshape(B, S, D_MODEL)
    return o @ w_o


def multiscale_pool_mixer(h, w_in, w_grp, scale):
    B, S, _ = h.shape
    u = (h @ w_in).reshape(B, S, N_POOL_GROUPS, POOL_GROUP)
    uf = u.astype(jnp.float32)
    c = jnp.cumsum(uf, axis=1)
    pos = jnp.arange(S)
    outs = []
    for gi, w in enumerate(POOL_WINDOWS):
        cg = c[:, :, gi]
        cpad = jnp.pad(cg, ((0, 0), (w, 0), (0, 0)))
        wsum = cpad[:, w:] - cpad[:, :S]
        cnt = jnp.minimum(pos + 1, w).astype(jnp.float32)
        outs.append(wsum / cnt[None, :, None] - uf[:, :, gi])
    pooled = jnp.stack(outs, axis=2).astype(h.dtype)
    y = jnp.einsum('bsgc,gcd->bsgd', pooled, w_grp).reshape(B, S, D_MODEL)
    return y * scale


def setup_inputs(seed: int = 0) -> dict:
    key = jax.random.key(seed)
    ks = iter(jax.random.split(key, 32))
    n_a = (DEPTH + 1) // 2
    n_b = DEPTH // 2
    f32 = jnp.float32

    def w(shape, fan_in):
        return jax.random.normal(next(ks), shape, f32) * fan_in ** -0.5

    def gain(shape):
        return 1.0 + 0.05 * jax.random.normal(next(ks), shape, f32)

    return {
        "x": jax.random.normal(next(ks), (BATCH, SEQ, D_MODEL), f32),
        "p": jax.random.normal(next(ks), (DEPTH, BATCH, SEQ, PLE_DIM), f32),
        "norm_ffn1": gain((DEPTH, D_MODEL)),
        "w_ffn1_gu": w((DEPTH, D_MODEL, 2 * D_FF), D_MODEL),
        "w_ffn1_down": w((DEPTH, D_FF, D_MODEL), D_FF),
        "norm_mix": gain((DEPTH, D_MODEL)),
        "w_qkv": w((n_a, D_MODEL, 3 * D_MODEL), D_MODEL),
        "q_norm": gain((n_a, HEAD_DIM)),
        "k_norm": gain((n_a, HEAD_DIM)),
        "w_o": w((n_a, D_MODEL, D_MODEL), D_MODEL),
        "w_pool_in": w((n_b, D_MODEL, D_MODEL), D_MODEL),
        "w_pool_grp": w((n_b, N_POOL_GROUPS, POOL_GROUP, POOL_GROUP), POOL_GROUP),
        "pool_scale": gain((n_b, D_MODEL)),
        "norm_ffn2": gain((DEPTH, D_MODEL)),
        "w_ffn2_gu": w((DEPTH, D_MODEL, 2 * D_FF), D_MODEL),
        "w_ffn2_down": w((DEPTH, D_FF, D_MODEL), D_FF),
        "norm_ple": gain((DEPTH, D_MODEL)),
        "w_ple_gate": w((DEPTH, D_MODEL, D_MODEL), D_MODEL),
        "w_ple_proj": w((DEPTH, PLE_DIM, D_MODEL), PLE_DIM),
    }


def reference(x, p, norm_ffn1, w_ffn1_gu, w_ffn1_down, norm_mix, w_qkv, q_norm,
              k_norm, w_o, w_pool_in, w_pool_grp, pool_scale, norm_ffn2,
              w_ffn2_gu, w_ffn2_down, norm_ple, w_ple_gate, w_ple_proj):
    for i in range(DEPTH):
        x = x + 0.5 * swiglu(rms_norm(x, norm_ffn1[i]), w_ffn1_gu[i], w_ffn1_down[i])
        h = rms_norm(x, norm_mix[i])
        j = i // N_MIXERS
        if i % N_MIXERS == 0:
            mix = stick_breaking_attention(h, w_qkv[j], q_norm[j], k_norm[j], w_o[j])
        else:
            mix = multiscale_pool_mixer(h, w_pool_in[j], w_pool_grp[j], pool_scale[j])
        x = x + mix
        x = x + 0.5 * swiglu(rms_norm(x, norm_ffn2[i]), w_ffn2_gu[i], w_ffn2_down[i])
        gate = jax.nn.sigmoid(rms_norm(x, norm_ple[i]) @ w_ple_gate[i])
        x = x + gate * (p[i] @ w_ple_proj[i])
    return x
```

```python
import functools

import jax
import jax.numpy as jnp
from jax import lax
from jax.experimental import pallas as pl
from jax.experimental.pallas import tpu as pltpu

D_MODEL = 1024
N_HEADS = 16
HEAD_DIM = D_MODEL // N_HEADS
D_FF = 2816
PLE_DIM = 256
POOL_WINDOWS = (2, 4, 8, 16)
POOL_GROUP = D_MODEL // len(POOL_WINDOWS)
EPS = 1e-6

LANES = 128
HEADS_PER_BLOCK = LANES // HEAD_DIM
FF_CHUNK = 256
N_FF_CHUNKS = D_FF // FF_CHUNK
ROW_TILE = 512
ATT_TQ = 128
ATT_TK = 128
POOL_HALO = 16
EXP_ZERO_BELOW = -104.0
VMEM_LIMIT = 56 * 1024 * 1024

F32 = jnp.float32
BF16 = jnp.bfloat16


def _rms(x, g):
    ms = jnp.mean(x * x, axis=-1, keepdims=True)
    return x * lax.rsqrt(ms + EPS) * g


def _params(*sem):
    return pltpu.CompilerParams(dimension_semantics=sem, vmem_limit_bytes=VMEM_LIMIT)


def _resident(shape, index_map):
    return pl.BlockSpec(shape, index_map, pipeline_mode=pl.Buffered(1))


def _ffn_kernel(x_ref, g_ref, wgu_ref, wd_ref, o_ref, a_ref):
    x = x_ref[...]
    h = _rms(x, g_ref[...]).astype(BF16)
    for c in range(N_FF_CHUNKS):
        lo = c * FF_CHUNK
        gate = jnp.dot(h, wgu_ref[:, lo:lo + FF_CHUNK], preferred_element_type=F32)
        up = jnp.dot(h, wgu_ref[:, D_FF + lo:D_FF + lo + FF_CHUNK], preferred_element_type=F32)
        a_ref[:, lo:lo + FF_CHUNK] = (gate * jax.nn.sigmoid(gate) * up).astype(BF16)
    y = jnp.dot(a_ref[...], wd_ref[...], preferred_element_type=F32)
    o_ref[...] = x + 0.5 * y


def _ffn(x, gain, w_gu, w_down, layer):
    t = x.shape[0]
    return pl.pallas_call(
        _ffn_kernel,
        grid=(t // ROW_TILE,),
        in_specs=[
            pl.BlockSpec((ROW_TILE, D_MODEL), lambda m: (m, 0)),
            _resident((None, 1, D_MODEL), lambda m: (layer, 0, 0)),
            _resident((None, D_MODEL, 2 * D_FF), lambda m: (layer, 0, 0)),
            _resident((None, D_FF, D_MODEL), lambda m: (layer, 0, 0)),
        ],
        out_specs=pl.BlockSpec((ROW_TILE, D_MODEL), lambda m: (m, 0)),
        out_shape=jax.ShapeDtypeStruct((t, D_MODEL), F32),
        scratch_shapes=[pltpu.VMEM((ROW_TILE, D_FF), BF16)],
        compiler_params=_params("arbitrary"),
        name="ffn",
    )(x, gain, w_gu, w_down)


def _split_hi_lo(x):
    hi = lax.bitcast_convert_type(
        lax.bitcast_convert_type(x, jnp.uint32) & jnp.uint32(0xFFFF0000), F32)
    return hi.astype(BF16), (x - hi).astype(BF16)


def _head_norm(y, gain, avg):
    cols = []
    for c in range(D_MODEL // LANES):
        blk = y[:, c * LANES:(c + 1) * LANES]
        hi, lo = _split_hi_lo(blk * blk)
        ms = jnp.dot(jnp.concatenate([hi, lo], axis=1), avg, preferred_element_type=F32)
        cols.append(blk * lax.rsqrt(ms + EPS))
    return jnp.concatenate(cols, axis=1) * gain


def _qkv_kernel(x_ref, g_ref, w_ref, qg_ref, kg_ref, avg_ref, q_ref, k_ref, v_ref):
    h = _rms(x_ref[...], g_ref[...]).astype(BF16)
    avg = avg_ref[...]
    q = jnp.dot(h, w_ref[:, 0:D_MODEL], preferred_element_type=F32)
    q_ref[...] = (_head_norm(q, qg_ref[...], avg) * (HEAD_DIM ** -0.5)).astype(BF16)
    k = jnp.dot(h, w_ref[:, D_MODEL:2 * D_MODEL], preferred_element_type=F32)
    k_ref[...] = _head_norm(k, kg_ref[...], avg).astype(BF16)
    v = jnp.dot(h, w_ref[:, 2 * D_MODEL:3 * D_MODEL], preferred_element_type=F32)
    v_ref[...] = v.astype(BF16)


def _qkv(x, gain, w_qkv, q_gain, k_gain, avg, layer, j):
    t = x.shape[0]
    row = pl.BlockSpec((ROW_TILE, D_MODEL), lambda m: (m, 0))
    out = jax.ShapeDtypeStruct((t, D_MODEL), BF16)
    return pl.pallas_call(
        _qkv_kernel,
        grid=(t // ROW_TILE,),
        in_specs=[
            row,
            _resident((None, 1, D_MODEL), lambda m: (layer, 0, 0)),
            _resident((None, D_MODEL, 3 * D_MODEL), lambda m: (j, 0, 0)),
            _resident((None, 1, D_MODEL), lambda m: (j, 0, 0)),
            _resident((None, 1, D_MODEL), lambda m: (j, 0, 0)),
            _resident((2 * LANES, LANES), lambda m: (0, 0)),
        ],
        out_specs=[row, row, row],
        out_shape=[out, out, out],
        compiler_params=_params("arbitrary"),
        name="qkv",
    )(x, gain, w_qkv, q_gain, k_gain, avg)


def _attn_kernel(q_ref, k_ref, v_ref, tri_ref, o_ref):
    qi = pl.program_id(2)
    q = q_ref[...]
    lane = lax.broadcasted_iota(jnp.int32, (ATT_TQ, LANES), 1)
    first_head = lane < HEAD_DIM
    qz = jnp.zeros_like(q)
    q2 = jnp.concatenate([jnp.where(first_head, q, qz), jnp.where(first_head, qz, q)], axis=0)
    row = lax.broadcasted_iota(jnp.int32, (2 * ATT_TQ, ATT_TK), 0) & (ATT_TQ - 1)
    col = lax.broadcasted_iota(jnp.int32, (2 * ATT_TQ, ATT_TK), 1)
    below_diag = col < row

    def tile(kt, carry, acc, masked):
        ks = pl.multiple_of(kt * ATT_TK, ATT_TK)
        k = k_ref[pl.ds(ks, ATT_TK), :]
        v = v_ref[pl.ds(ks, ATT_TK), :]
        z = lax.dot_general(q2, k, (((1,), (1,)), ((), ())), preferred_element_type=F32)
        nz = -z
        log_stay = jnp.minimum(nz, 0.0) - jnp.log(1.0 + jnp.exp(jnp.minimum(z, nz)))
        if masked:
            log_stay = jnp.where(below_diag, log_stay, 0.0)
        hi, lo = _split_hi_lo(log_stay)
        sums = jnp.dot(jnp.concatenate([hi, lo], axis=1), tri_ref[...], preferred_element_type=F32)
        w = jnp.exp(z + sums[:, :ATT_TK] + carry)
        if masked:
            w = jnp.where(below_diag, w, 0.0)
        wb = w.astype(BF16)
        w2 = jnp.concatenate([wb[:ATT_TQ], wb[ATT_TQ:]], axis=1)
        vz = jnp.zeros_like(v)
        v2 = jnp.concatenate([jnp.where(first_head, v, vz), jnp.where(first_head, vz, v)], axis=0)
        acc = acc + jnp.dot(w2, v2, preferred_element_type=F32)
        return carry + sums[:, ATT_TK:], acc

    carry = jnp.zeros((2 * ATT_TQ, LANES), F32)
    acc = jnp.zeros((ATT_TQ, LANES), F32)
    carry, acc = tile(qi, carry, acc, True)

    def cond(state):
        kt, live, _, _ = state
        return jnp.logical_and(kt >= 0, live)

    def body(state):
        kt, _, carry, acc = state
        carry, acc = tile(kt, carry, acc, False)
        return kt - 1, jnp.max(carry) > EXP_ZERO_BELOW, carry, acc

    _, _, _, acc = lax.while_loop(
        cond, body, (qi - 1, jnp.max(carry) > EXP_ZERO_BELOW, carry, acc))
    o_ref[...] = acc.astype(BF16)


def _attention(q, k, v, tri, batch, seq):
    t = q.shape[0]
    n_q = seq // ATT_TQ
    q_spec = pl.BlockSpec((ATT_TQ, LANES), lambda b, h, i: (b * n_q + i, h))
    kv_spec = pl.BlockSpec((seq, LANES), lambda b, h, i: (b, h))
    return pl.pallas_call(
        _attn_kernel,
        grid=(batch, N_HEADS // HEADS_PER_BLOCK, n_q),
        in_specs=[q_spec, kv_spec, kv_spec,
                  _resident((2 * ATT_TK, ATT_TK + LANES), lambda b, h, i: (0, 0))],
        out_specs=q_spec,
        out_shape=jax.ShapeDtypeStruct((t, D_MODEL), BF16),
        compiler_params=_params("arbitrary", "arbitrary", "arbitrary"),
        name="attn",
    )(q, k, v, tri)


def _out_proj_kernel(x_ref, o_ref, w_ref, y_ref):
    y_ref[...] = x_ref[...] + jnp.dot(o_ref[...], w_ref[...], preferred_element_type=F32)


def _out_proj(x, o, w_o, j):
    t = x.shape[0]
    row = pl.BlockSpec((ROW_TILE, D_MODEL), lambda m: (m, 0))
    return pl.pallas_call(
        _out_proj_kernel,
        grid=(t // ROW_TILE,),
        in_specs=[row, row, _resident((None, D_MODEL, D_MODEL), lambda m: (j, 0, 0))],
        out_specs=row,
        out_shape=jax.ShapeDtypeStruct((t, D_MODEL), F32),
        compiler_params=_params("arbitrary"),
        name="out_proj",
    )(x, o, w_o)


def _pool_kernel(x_ref, g_ref, win_ref, wgrp_ref, sc_ref, o_ref, hist_ref):
    si = pl.program_id(1)
    x = x_ref[...]
    h = _rms(x, g_ref[...]).astype(BF16)
    u = jnp.dot(h, win_ref[...], preferred_element_type=F32)

    @pl.when(si == 0)
    def _():
        hist_ref[0:POOL_HALO, :] = jnp.zeros((POOL_HALO, D_MODEL), F32)

    @pl.when(si > 0)
    def _():
        hist_ref[0:POOL_HALO, :] = hist_ref[ROW_TILE:ROW_TILE + POOL_HALO, :]

    hist_ref[POOL_HALO:POOL_HALO + ROW_TILE, :] = u
    pos = si * ROW_TILE + lax.broadcasted_iota(jnp.int32, (ROW_TILE, 1), 0)
    for gi, win in enumerate(POOL_WINDOWS):
        lo = gi * POOL_GROUP
        ug = u[:, lo:lo + POOL_GROUP]
        wsum = ug
        for d in range(1, win):
            wsum = wsum + hist_ref[POOL_HALO - d:POOL_HALO - d + ROW_TILE, lo:lo + POOL_GROUP]
        cnt = jnp.minimum(pos + 1, win).astype(F32)
        pooled = wsum / cnt - ug
        y = jnp.dot(pooled.astype(BF16), wgrp_ref[gi], preferred_element_type=F32)
        o_ref[:, lo:lo + POOL_GROUP] = x[:, lo:lo + POOL_GROUP] + y * sc_ref[:, lo:lo + POOL_GROUP]


def _pool(x, gain, w_in, w_grp, scale, batch, seq, layer, j):
    t = x.shape[0]
    n_s = seq // ROW_TILE
    row = pl.BlockSpec((ROW_TILE, D_MODEL), lambda b, s: (b * n_s + s, 0))
    n_g = len(POOL_WINDOWS)
    return pl.pallas_call(
        _pool_kernel,
        grid=(batch, n_s),
        in_specs=[
            row,
            _resident((None, 1, D_MODEL), lambda b, s: (layer, 0, 0)),
            _resident((None, D_MODEL, D_MODEL), lambda b, s: (j, 0, 0)),
            _resident((None, n_g, POOL_GROUP, POOL_GROUP), lambda b, s: (j, 0, 0, 0)),
            _resident((None, 1, D_MODEL), lambda b, s: (j, 0, 0)),
        ],
        out_specs=row,
        out_shape=jax.ShapeDtypeStruct((t, D_MODEL), F32),
        scratch_shapes=[pltpu.VMEM((ROW_TILE + POOL_HALO, D_MODEL), F32)],
        compiler_params=_params("arbitrary", "arbitrary"),
        name="pool",
    )(x, gain, w_in, w_grp, scale)


def _ple_kernel(x_ref, p_ref, g_ref, wg_ref, wp_ref, o_ref):
    x = x_ref[...]
    h = _rms(x, g_ref[...]).astype(BF16)
    gate = jax.nn.sigmoid(jnp.dot(h, wg_ref[...], preferred_element_type=F32))
    proj = jnp.dot(p_ref[...].astype(BF16), wp_ref[...], preferred_element_type=F32)
    o_ref[...] = x + gate * proj


def _ple(x, p, gain, w_gate, w_proj, layer):
    t = x.shape[0]
    row = pl.BlockSpec((ROW_TILE, D_MODEL), lambda m: (m, 0))
    return pl.pallas_call(
        _ple_kernel,
        grid=(t // ROW_TILE,),
        in_specs=[
            row,
            pl.BlockSpec((None, ROW_TILE, PLE_DIM), lambda m: (layer, m, 0)),
            _resident((None, 1, D_MODEL), lambda m: (layer, 0, 0)),
            _resident((None, D_MODEL, D_MODEL), lambda m: (layer, 0, 0)),
            _resident((None, PLE_DIM, D_MODEL), lambda m: (layer, 0, 0)),
        ],
        out_specs=row,
        out_shape=jax.ShapeDtypeStruct((t, D_MODEL), F32),
        compiler_params=_params("arbitrary"),
        name="ple",
    )(x, p, gain, w_gate, w_proj)


def _attention_constants():
    j = lax.broadcasted_iota(jnp.int32, (ATT_TK, ATT_TK), 0)
    s = lax.broadcasted_iota(jnp.int32, (ATT_TK, ATT_TK), 1)
    suffix = (j >= s).astype(BF16)
    half = jnp.concatenate([suffix, jnp.ones((ATT_TK, LANES), BF16)], axis=1)
    tri = jnp.concatenate([half, half], axis=0)
    r = lax.broadcasted_iota(jnp.int32, (LANES, LANES), 0) // HEAD_DIM
    c = lax.broadcasted_iota(jnp.int32, (LANES, LANES), 1) // HEAD_DIM
    blk = jnp.where(r == c, 1.0 / HEAD_DIM, 0.0).astype(BF16)
    avg = jnp.concatenate([blk, blk], axis=0)
    return tri, avg


def kernel(x, p, norm_ffn1, w_ffn1_gu, w_ffn1_down, norm_mix, w_qkv, q_norm, k_norm, w_o,
           w_pool_in, w_pool_grp, pool_scale, norm_ffn2, w_ffn2_gu, w_ffn2_down, norm_ple,
           w_ple_gate, w_ple_proj):
    batch, seq, _ = x.shape
    depth = norm_ffn1.shape[0]
    t = batch * seq
    bf = lambda w: w.astype(BF16)
    row3 = lambda g: g.reshape(g.shape[0], 1, g.shape[1])
    w1gu, w1d, w2gu, w2d = bf(w_ffn1_gu), bf(w_ffn1_down), bf(w_ffn2_gu), bf(w_ffn2_down)
    wqkv, wo, wpin, wpgrp = bf(w_qkv), bf(w_o), bf(w_pool_in), bf(w_pool_grp)
    wpg, wpp = bf(w_ple_gate), bf(w_ple_proj)
    n1, nm, n2, npl = row3(norm_ffn1), row3(norm_mix), row3(norm_ffn2), row3(norm_ple)
    qg = row3(jnp.tile(q_norm, (1, N_HEADS)))
    kg = row3(jnp.tile(k_norm, (1, N_HEADS)))
    psc = row3(pool_scale)
    tri, avg = _attention_constants()
    p2 = p.reshape(depth, t, PLE_DIM)

    xs = x.reshape(t, D_MODEL)
    for i in range(depth):
        xs = _ffn(xs, n1, w1gu, w1d, i)
        j = i // 2
        if i % 2 == 0:
            q, k, v = _qkv(xs, nm, wqkv, qg, kg, avg, i, j)
            o = _attention(q, k, v, tri, batch, seq)
            xs = _out_proj(xs, o, wo, j)
        else:
            xs = _pool(xs, nm, wpin, wpgrp, psc, batch, seq, i, j)
        xs = _ffn(xs, n2, w2gu, w2d, i)
        xs = _ple(xs, p2, npl, wpg, wpp, i)
    return xs.reshape(batch, seq, D_MODEL)
```

```python
import jax
import jax.numpy as jnp
from jax import lax
from jax.experimental import pallas as pl
from jax.experimental.pallas import tpu as pltpu

D_MODEL = 1024
N_HEADS = 16
HEAD_DIM = D_MODEL // N_HEADS
D_FF = 2816
PLE_DIM = 256
POOL_WINDOWS = (2, 4, 8, 16)
POOL_GROUP = D_MODEL // len(POOL_WINDOWS)
EPS = 1e-6

LANES = 128
HEADS_PER_BLOCK = LANES // HEAD_DIM
FF_CHUNK = 256
N_FF_CHUNKS = D_FF // FF_CHUNK
ROW_TILE = 512
ATT_TILE = 128
ATT_NSUB = 4
ATT_BLOCK = ATT_TILE * ATT_NSUB
ATT_STATIC = 3
POOL_HALO = 16
EXP_ZERO_BELOW = -104.0
VMEM_LIMIT = 56 * 1024 * 1024

F32 = jnp.float32
BF16 = jnp.bfloat16


def _rms(x, g):
    ms = jnp.mean(x * x, axis=-1, keepdims=True)
    return x * lax.rsqrt(ms + EPS) * g


def _params(*sem):
    return pltpu.CompilerParams(dimension_semantics=sem, vmem_limit_bytes=VMEM_LIMIT)


def _resident(shape, index_map):
    return pl.BlockSpec(shape, index_map, pipeline_mode=pl.Buffered(1))


def _layer_block(shape, layer):
    zeros = (0,) * len(shape)
    return _resident((None,) + tuple(shape), lambda *_: (layer,) + zeros)


def _split_hi_lo(x):
    hi = lax.bitcast_convert_type(
        lax.bitcast_convert_type(x, jnp.uint32) & jnp.uint32(0xFFFF0000), F32)
    return hi.astype(BF16), (x - hi).astype(BF16)


def _ffn_stage(x, g_ref, wgu_ref, wd_ref, a_ref):
    h = _rms(x, g_ref[...]).astype(BF16)
    for c in range(N_FF_CHUNKS):
        lo = c * FF_CHUNK
        gate = jnp.dot(h, wgu_ref[:, lo:lo + FF_CHUNK], preferred_element_type=F32)
        up = jnp.dot(h, wgu_ref[:, D_FF + lo:D_FF + lo + FF_CHUNK], preferred_element_type=F32)
        a_ref[:, lo:lo + FF_CHUNK] = (gate * jax.nn.sigmoid(gate) * up).astype(BF16)
    return x + 0.5 * jnp.dot(a_ref[...], wd_ref[...], preferred_element_type=F32)


def _ple_stage(x, p_ref, g_ref, wg_ref, wp_ref):
    h = _rms(x, g_ref[...]).astype(BF16)
    gate = jax.nn.sigmoid(jnp.dot(h, wg_ref[...], preferred_element_type=F32))
    proj = jnp.dot(p_ref[...].astype(BF16), wp_ref[...], preferred_element_type=F32)
    return x + gate * proj


def _head_norm(y, gain, avg):
    cols = []
    for c in range(D_MODEL // LANES):
        blk = y[:, c * LANES:(c + 1) * LANES]
        hi, lo = _split_hi_lo(blk * blk)
        ms = jnp.dot(jnp.concatenate([hi, lo], axis=1), avg, preferred_element_type=F32)
        cols.append(blk * lax.rsqrt(ms + EPS))
    return jnp.concatenate(cols, axis=1) * gain


def _qkv_stage(x, g_ref, w_ref, qg_ref, kg_ref, avg_ref, q_ref, k_ref, v_ref):
    h = _rms(x, g_ref[...]).astype(BF16)
    avg = avg_ref[...]
    q = jnp.dot(h, w_ref[:, 0:D_MODEL], preferred_element_type=F32)
    q_ref[...] = (_head_norm(q, qg_ref[...], avg) * (HEAD_DIM ** -0.5)).astype(BF16)
    k = jnp.dot(h, w_ref[:, D_MODEL:2 * D_MODEL], preferred_element_type=F32)
    k_ref[...] = _head_norm(k, kg_ref[...], avg).astype(BF16)
    v = jnp.dot(h, w_ref[:, 2 * D_MODEL:3 * D_MODEL], preferred_element_type=F32)
    v_ref[...] = v.astype(BF16)


def _pool_stage(x, si, g_ref, win_ref, wgrp_ref, sc_ref, o_ref, hist_ref):
    h = _rms(x, g_ref[...]).astype(BF16)
    u = jnp.dot(h, win_ref[...], preferred_element_type=F32)

    @pl.when(si == 0)
    def _():
        hist_ref[0:POOL_HALO, :] = jnp.zeros((POOL_HALO, D_MODEL), F32)

    @pl.when(si > 0)
    def _():
        hist_ref[0:POOL_HALO, :] = hist_ref[ROW_TILE:ROW_TILE + POOL_HALO, :]

    hist_ref[POOL_HALO:POOL_HALO + ROW_TILE, :] = u
    pos = si * ROW_TILE + lax.broadcasted_iota(jnp.int32, (ROW_TILE, 1), 0)
    for gi, win in enumerate(POOL_WINDOWS):
        lo = gi * POOL_GROUP
        ug = u[:, lo:lo + POOL_GROUP]
        wsum = ug
        for d in range(1, win):
            wsum = wsum + hist_ref[POOL_HALO - d:POOL_HALO - d + ROW_TILE, lo:lo + POOL_GROUP]
        cnt = jnp.minimum(pos + 1, win).astype(F32)
        pooled = wsum / cnt - ug
        y = jnp.dot(pooled.astype(BF16), wgrp_ref[gi], preferred_element_type=F32)
        o_ref[:, lo:lo + POOL_GROUP] = x[:, lo:lo + POOL_GROUP] + y * sc_ref[:, lo:lo + POOL_GROUP]


def _ffn_qkv_kernel(x_ref, g1_ref, wgu_ref, wd_ref, gm_ref, wqkv_ref, qg_ref, kg_ref, avg_ref,
                    x1_ref, q_ref, k_ref, v_ref, a_ref):
    x1 = _ffn_stage(x_ref[...], g1_ref, wgu_ref, wd_ref, a_ref)
    x1_ref[...] = x1
    _qkv_stage(x1, gm_ref, wqkv_ref, qg_ref, kg_ref, avg_ref, q_ref, k_ref, v_ref)


def _ffn_qkv(x, n1, w_gu, w_down, nm, w_qkv, q_gain, k_gain, avg, layer, j):
    t = x.shape[0]
    row = pl.BlockSpec((ROW_TILE, D_MODEL), lambda m: (m, 0))
    act = jax.ShapeDtypeStruct((t, D_MODEL), BF16)
    return pl.pallas_call(
        _ffn_qkv_kernel,
        grid=(t // ROW_TILE,),
        in_specs=[
            row,
            _layer_block((1, D_MODEL), layer),
            _layer_block((D_MODEL, 2 * D_FF), layer),
            _layer_block((D_FF, D_MODEL), layer),
            _layer_block((1, D_MODEL), layer),
            _layer_block((D_MODEL, 3 * D_MODEL), j),
            _layer_block((1, D_MODEL), j),
            _layer_block((1, D_MODEL), j),
            _resident((2 * LANES, LANES), lambda m: (0, 0)),
        ],
        out_specs=[row, row, row, row],
        out_shape=[jax.ShapeDtypeStruct((t, D_MODEL), F32), act, act, act],
        scratch_shapes=[pltpu.VMEM((ROW_TILE, D_FF), BF16)],
        compiler_params=_params("arbitrary"),
        name="ffn_qkv",
    )(x, n1, w_gu, w_down, nm, w_qkv, q_gain, k_gain, avg)


def _ffn_pool_kernel(x_ref, g1_ref, wgu_ref, wd_ref, gm_ref, win_ref, wgrp_ref, sc_ref,
                     o_ref, a_ref, hist_ref):
    x1 = _ffn_stage(x_ref[...], g1_ref, wgu_ref, wd_ref, a_ref)
    _pool_stage(x1, pl.program_id(1), gm_ref, win_ref, wgrp_ref, sc_ref, o_ref, hist_ref)


def _ffn_pool(x, n1, w_gu, w_down, nm, w_in, w_grp, scale, batch, seq, layer, j):
    t = x.shape[0]
    n_s = seq // ROW_TILE
    row = pl.BlockSpec((ROW_TILE, D_MODEL), lambda b, s: (b * n_s + s, 0))
    n_g = len(POOL_WINDOWS)
    return pl.pallas_call(
        _ffn_pool_kernel,
        grid=(batch, n_s),
        in_specs=[
            row,
            _layer_block((1, D_MODEL), layer),
            _layer_block((D_MODEL, 2 * D_FF), layer),
            _layer_block((D_FF, D_MODEL), layer),
            _layer_block((1, D_MODEL), layer),
            _layer_block((D_MODEL, D_MODEL), j),
            _layer_block((n_g, POOL_GROUP, POOL_GROUP), j),
            _layer_block((1, D_MODEL), j),
        ],
        out_specs=row,
        out_shape=jax.ShapeDtypeStruct((t, D_MODEL), F32),
        scratch_shapes=[pltpu.VMEM((ROW_TILE, D_FF), BF16),
                        pltpu.VMEM((ROW_TILE + POOL_HALO, D_MODEL), F32)],
        compiler_params=_params("arbitrary", "arbitrary"),
        name="ffn_pool",
    )(x, n1, w_gu, w_down, nm, w_in, w_grp, scale)


def _proj_ffn_ple_kernel(x_ref, o_ref, wo_ref, g2_ref, wgu_ref, wd_ref, p_ref, gp_ref, wg_ref,
                         wp_ref, y_ref, a_ref):
    x = x_ref[...] + jnp.dot(o_ref[...], wo_ref[...], preferred_element_type=F32)
    x = _ffn_stage(x, g2_ref, wgu_ref, wd_ref, a_ref)
    y_ref[...] = _ple_stage(x, p_ref, gp_ref, wg_ref, wp_ref)


def _ffn_ple_kernel(x_ref, g2_ref, wgu_ref, wd_ref, p_ref, gp_ref, wg_ref, wp_ref, y_ref, a_ref):
    x = _ffn_stage(x_ref[...], g2_ref, wgu_ref, wd_ref, a_ref)
    y_ref[...] = _ple_stage(x, p_ref, gp_ref, wg_ref, wp_ref)


def _ffn_ple(x, o, w_o, n2, w_gu, w_down, p, npl, w_gate, w_proj, layer, j):
    t = x.shape[0]
    row = pl.BlockSpec((ROW_TILE, D_MODEL), lambda m: (m, 0))
    tail_specs = [
        _layer_block((1, D_MODEL), layer),
        _layer_block((D_MODEL, 2 * D_FF), layer),
        _layer_block((D_FF, D_MODEL), layer),
        pl.BlockSpec((None, ROW_TILE, PLE_DIM), lambda m: (layer, m, 0)),
        _layer_block((1, D_MODEL), layer),
        _layer_block((D_MODEL, D_MODEL), layer),
        _layer_block((PLE_DIM, D_MODEL), layer),
    ]
    tail_args = (n2, w_gu, w_down, p, npl, w_gate, w_proj)
    if o is None:
        body, specs, args, name = _ffn_ple_kernel, [row] + tail_specs, (x,) + tail_args, "ffn_ple"
    else:
        body, name = _proj_ffn_ple_kernel, "proj_ffn_ple"
        specs = [row, row, _layer_block((D_MODEL, D_MODEL), j)] + tail_specs
        args = (x, o, w_o) + tail_args
    return pl.pallas_call(
        body,
        grid=(t // ROW_TILE,),
        in_specs=specs,
        out_specs=row,
        out_shape=jax.ShapeDtypeStruct((t, D_MODEL), F32),
        scratch_shapes=[pltpu.VMEM((ROW_TILE, D_FF), BF16)],
        compiler_params=_params("arbitrary"),
        name=name,
    )(*args)


def _tree_max(blocks):
    rows = [b[r:r + 8, c:c + LANES] for b in blocks
            for r in range(0, b.shape[0], 8) for c in range(0, b.shape[1], LANES)]
    while len(rows) > 1:
        rows = [jnp.maximum(a, b) for a, b in zip(rows[0::2], rows[1::2])] + rows[len(rows) & ~1:]
    return jnp.max(rows[0])


def _attn_kernel(q_ref, k_ref, v_ref, suf_ref, o_ref, carry_ref, acc_ref, live_ref):
    qb = pl.program_id(2)
    lane = lax.broadcasted_iota(jnp.int32, (ATT_TILE, LANES), 1)
    first_head = lane < HEAD_DIM
    row = lax.broadcasted_iota(jnp.int32, (ATT_TILE, 2 * ATT_TILE), 0)
    col = lax.broadcasted_iota(jnp.int32, (ATT_TILE, 2 * ATT_TILE), 1) & (ATT_TILE - 1)
    below_diag = col < row
    contract_last = (((1,), (1,)), ((), ()))

    def per_head(x):
        xz = jnp.zeros_like(x)
        return jnp.concatenate([jnp.where(first_head, x, xz), jnp.where(first_head, xz, x)], axis=0)

    def key_tile(kt):
        ks = pl.multiple_of(kt * ATT_TILE, ATT_TILE)
        return per_head(k_ref[pl.ds(ks, ATT_TILE), :]), per_head(v_ref[pl.ds(ks, ATT_TILE), :])

    def softplus(z):
        return jnp.maximum(z, 0.0) + jnp.log(1.0 + jnp.exp(-jnp.abs(z)))

    def suffix_lhs(sp):
        hi, lo = _split_hi_lo(sp)
        return jnp.concatenate([hi, lo], axis=1)

    def row_totals(neg_sums):
        return jnp.concatenate(
            [jnp.broadcast_to(neg_sums[:, 0:1], (ATT_TILE, ATT_TILE)),
             jnp.broadcast_to(neg_sums[:, ATT_TILE:ATT_TILE + 1], (ATT_TILE, ATT_TILE))], axis=1)

    def static_part(first_block):
        halo = 0 if first_block else ATT_STATIC - 1
        tiles = range(-halo, ATT_NSUB)
        users = {j: [i for i in range(ATT_NSUB) if 0 <= i - j < ATT_STATIC] for j in tiles}
        kv = {j: key_tile(qb * ATT_NSUB + j) for j in tiles}
        z = {}
        for j in tiles:
            i0, n = users[j][0], len(users[j])
            zz = lax.dot_general(q_ref[i0 * ATT_TILE:(i0 + n) * ATT_TILE, :], kv[j][0],
                                 contract_last, preferred_element_type=F32)
            for m, i in enumerate(users[j]):
                z[i, j] = zz[m * ATT_TILE:(m + 1) * ATT_TILE]
        pairs = sorted(z)
        lhs = []
        for i, j in pairs:
            sp = softplus(z[i, j])
            if i == j:
                sp = jnp.where(below_diag, sp, 0.0)
            lhs.append(suffix_lhs(sp))
        neg_sums = jnp.dot(jnp.concatenate(lhs, axis=0), suf_ref[...], preferred_element_type=F32)
        wb = {}
        finals = []
        for i in range(ATT_NSUB):
            carry = None
            for j in range(i, max(i - ATT_STATIC, -halo - 1), -1):
                p = pairs.index((i, j))
                ns = neg_sums[p * ATT_TILE:(p + 1) * ATT_TILE]
                arg = z[i, j] + ns
                if carry is not None:
                    arg = arg + carry
                w = jnp.exp(arg)
                if i == j:
                    w = jnp.where(below_diag, w, 0.0)
                wb[i, j] = w.astype(BF16)
                tot = row_totals(ns)
                carry = tot if carry is None else carry + tot
            carry_ref[i] = carry
            finals.append(carry)
        live_ref[0] = (_tree_max(finals) > EXP_ZERO_BELOW).astype(jnp.int32)
        acc = [None] * ATT_NSUB
        for j in tiles:
            out = jnp.dot(jnp.concatenate([wb[i, j] for i in users[j]], axis=0), kv[j][1],
                          preferred_element_type=F32)
            for m, i in enumerate(users[j]):
                part = out[m * ATT_TILE:(m + 1) * ATT_TILE]
                acc[i] = part if acc[i] is None else acc[i] + part
        for i in range(ATT_NSUB):
            acc_ref[i] = acc[i]
            o_ref[i * ATT_TILE:(i + 1) * ATT_TILE, :] = acc[i].astype(BF16)

    @pl.when(qb == 0)
    def _():
        static_part(True)

    @pl.when(qb > 0)
    def _():
        static_part(False)

    @pl.when(live_ref[0] > 0)
    def _():
        for i in range(ATT_NSUB):
            q = q_ref[i * ATT_TILE:(i + 1) * ATT_TILE, :]

            def cond(state):
                kt, live, _, _ = state
                return jnp.logical_and(kt >= 0, live)

            def body(state):
                kt, _, carry, acc = state
                kk, vv = key_tile(kt)
                z = lax.dot_general(q, kk, contract_last, preferred_element_type=F32)
                ns = jnp.dot(suffix_lhs(softplus(z)), suf_ref[...], preferred_element_type=F32)
                w = jnp.exp(z + ns + carry).astype(BF16)
                acc = acc + jnp.dot(w, vv, preferred_element_type=F32)
                carry = carry + row_totals(ns)
                return kt - 1, jnp.max(carry) > EXP_ZERO_BELOW, carry, acc

            carry = carry_ref[i]
            state = (qb * ATT_NSUB + i - ATT_STATIC, jnp.max(carry) > EXP_ZERO_BELOW,
                     carry, acc_ref[i])
            acc = lax.while_loop(cond, body, state)[3]
            o_ref[i * ATT_TILE:(i + 1) * ATT_TILE, :] = acc.astype(BF16)


def _attention(q, k, v, suffix, batch, seq):
    t = q.shape[0]
    n_q = seq // ATT_BLOCK
    q_spec = pl.BlockSpec((ATT_BLOCK, LANES), lambda b, h, i: (b * n_q + i, h))
    kv_spec = pl.BlockSpec((seq, LANES), lambda b, h, i: (b, h))
    return pl.pallas_call(
        _attn_kernel,
        grid=(batch, N_HEADS // HEADS_PER_BLOCK, n_q),
        in_specs=[q_spec, kv_spec, kv_spec,
                  _resident((4 * ATT_TILE, 2 * ATT_TILE), lambda b, h, i: (0, 0))],
        out_specs=q_spec,
        out_shape=jax.ShapeDtypeStruct((t, D_MODEL), BF16),
        scratch_shapes=[pltpu.VMEM((ATT_NSUB, ATT_TILE, 2 * ATT_TILE), F32),
                        pltpu.VMEM((ATT_NSUB, ATT_TILE, LANES), F32),
                        pltpu.SMEM((1,), jnp.int32)],
        compiler_params=_params("arbitrary", "arbitrary", "arbitrary"),
        name="attn",
    )(q, k, v, suffix)


def _attention_constants():
    j = lax.broadcasted_iota(jnp.int32, (2 * ATT_TILE, 2 * ATT_TILE), 0)
    s = lax.broadcasted_iota(jnp.int32, (2 * ATT_TILE, 2 * ATT_TILE), 1)
    same_head = (j // ATT_TILE) == (s // ATT_TILE)
    half = jnp.where(jnp.logical_and(same_head, j >= s), -1.0, 0.0).astype(BF16)
    suffix = jnp.concatenate([half, half], axis=0)
    r = lax.broadcasted_iota(jnp.int32, (LANES, LANES), 0) // HEAD_DIM
    c = lax.broadcasted_iota(jnp.int32, (LANES, LANES), 1) // HEAD_DIM
    blk = jnp.where(r == c, 1.0 / HEAD_DIM, 0.0).astype(BF16)
    avg = jnp.concatenate([blk, blk], axis=0)
    return suffix, avg


def kernel(x, p, norm_ffn1, w_ffn1_gu, w_ffn1_down, norm_mix, w_qkv, q_norm, k_norm, w_o,
           w_pool_in, w_pool_grp, pool_scale, norm_ffn2, w_ffn2_gu, w_ffn2_down, norm_ple,
           w_ple_gate, w_ple_proj):
    batch, seq, _ = x.shape
    depth = norm_ffn1.shape[0]
    t = batch * seq
    bf = lambda w: w.astype(BF16)
    row3 = lambda g: g.reshape(g.shape[0], 1, g.shape[1])
    w1gu, w1d, w2gu, w2d = bf(w_ffn1_gu), bf(w_ffn1_down), bf(w_ffn2_gu), bf(w_ffn2_down)
    wqkv, wo, wpin, wpgrp = bf(w_qkv), bf(w_o), bf(w_pool_in), bf(w_pool_grp)
    wpg, wpp = bf(w_ple_gate), bf(w_ple_proj)
    n1, nm, n2, npl = row3(norm_ffn1), row3(norm_mix), row3(norm_ffn2), row3(norm_ple)
    qg = row3(jnp.tile(q_norm, (1, N_HEADS)))
    kg = row3(jnp.tile(k_norm, (1, N_HEADS)))
    psc = row3(pool_scale)
    suffix, avg = _attention_constants()
    p2 = p.reshape(depth, t, PLE_DIM)

    xs = x.reshape(t, D_MODEL)
    for i in range(depth):
        j = i // 2
        if i % 2 == 0:
            x1, q, k, v = _ffn_qkv(xs, n1, w1gu, w1d, nm, wqkv, qg, kg, avg, i, j)
            o = _attention(q, k, v, suffix, batch, seq)
        else:
            x1 = _ffn_pool(xs, n1, w1gu, w1d, nm, wpin, wpgrp, psc, batch, seq, i, j)
            o = None
        xs = _ffn_ple(x1, o, wo, n2, w2gu, w2d, p2, npl, wpg, wpp, i, j)
    return xs.reshape(batch, seq, D_MODEL)
```

```python
import jax
import jax.numpy as jnp
from jax import lax
from jax.experimental import pallas as pl
from jax.experimental.pallas import tpu as pltpu

D_MODEL = 1024
N_HEADS = 16
HEAD_DIM = D_MODEL // N_HEADS
D_FF = 2816
PLE_DIM = 256
POOL_WINDOWS = (2, 4, 8, 16)
POOL_GROUP = D_MODEL // len(POOL_WINDOWS)
EPS = 1e-6

LANES = 128
HEADS_PER_BLOCK = LANES // HEAD_DIM
FF_CHUNK = 256
N_FF_CHUNKS = D_FF // FF_CHUNK
ROW_TILE = 512
ATT_TILE = 128
ATT_NSUB = 8
ATT_BLOCK = ATT_TILE * ATT_NSUB
ATT_STATIC = 3
ATT_TRIM = 32
ATT_GROUP = 2
ATT_AHEAD = 2
MASKED_SCORE = 1e30
LOG2E = 1.4426950408889634
POOL_HALO = 16
EXP_ZERO_BELOW = -104.0
VMEM_LIMIT = 56 * 1024 * 1024

F32 = jnp.float32
BF16 = jnp.bfloat16


def _rms(x, g):
    ms = jnp.mean(x * x, axis=-1, keepdims=True)
    return x * lax.rsqrt(ms + EPS) * g


def _params(*sem):
    return pltpu.CompilerParams(dimension_semantics=sem, vmem_limit_bytes=VMEM_LIMIT)


def _resident(shape, index_map):
    return pl.BlockSpec(shape, index_map, pipeline_mode=pl.Buffered(1))


def _layer_block(shape, layer):
    zeros = (0,) * len(shape)
    return _resident((None,) + tuple(shape), lambda *_: (layer,) + zeros)


def _split_hi_lo(x):
    hi = lax.bitcast_convert_type(
        lax.bitcast_convert_type(x, jnp.uint32) & jnp.uint32(0xFFFF0000), F32)
    return hi.astype(BF16), (x - hi).astype(BF16)


def _ffn_stage(x, g_ref, wgu_ref, wd_ref, a_ref):
    h = _rms(x, g_ref[...]).astype(BF16)
    for c in range(N_FF_CHUNKS):
        lo = c * FF_CHUNK
        gate = jnp.dot(h, wgu_ref[:, lo:lo + FF_CHUNK], preferred_element_type=F32)
        up = jnp.dot(h, wgu_ref[:, D_FF + lo:D_FF + lo + FF_CHUNK], preferred_element_type=F32)
        a_ref[:, lo:lo + FF_CHUNK] = (gate * jax.nn.sigmoid(gate) * up).astype(BF16)
    return x + 0.5 * jnp.dot(a_ref[...], wd_ref[...], preferred_element_type=F32)


def _ple_stage(x, p_ref, g_ref, wg_ref, wp_ref):
    h = _rms(x, g_ref[...]).astype(BF16)
    gate = jax.nn.sigmoid(jnp.dot(h, wg_ref[...], preferred_element_type=F32))
    proj = jnp.dot(p_ref[...].astype(BF16), wp_ref[...], preferred_element_type=F32)
    return x + gate * proj


def _head_norm(y, gain, avg):
    cols = []
    for c in range(D_MODEL // LANES):
        blk = y[:, c * LANES:(c + 1) * LANES]
        hi, lo = _split_hi_lo(blk * blk)
        ms = jnp.dot(jnp.concatenate([hi, lo], axis=1), avg, preferred_element_type=F32)
        cols.append(blk * lax.rsqrt(ms + EPS))
    return jnp.concatenate(cols, axis=1) * gain


def _qkv_stage(x, g_ref, w_ref, qg_ref, kg_ref, avg_ref, q_ref, k_ref, v_ref):
    h = _rms(x, g_ref[...]).astype(BF16)
    avg = avg_ref[...]
    q = jnp.dot(h, w_ref[:, 0:D_MODEL], preferred_element_type=F32)
    q_ref[...] = (_head_norm(q, qg_ref[...], avg) * (HEAD_DIM ** -0.5)).astype(BF16)
    k = jnp.dot(h, w_ref[:, D_MODEL:2 * D_MODEL], preferred_element_type=F32)
    k_ref[...] = _head_norm(k, kg_ref[...], avg).astype(BF16)
    v = jnp.dot(h, w_ref[:, 2 * D_MODEL:3 * D_MODEL], preferred_element_type=F32)
    v_ref[...] = v.astype(BF16)


def _pool_stage(x, si, g_ref, win_ref, wgrp_ref, sc_ref, o_ref, hist_ref):
    h = _rms(x, g_ref[...]).astype(BF16)
    u = jnp.dot(h, win_ref[...], preferred_element_type=F32)

    @pl.when(si == 0)
    def _():
        hist_ref[0:POOL_HALO, :] = jnp.zeros((POOL_HALO, D_MODEL), F32)

    @pl.when(si > 0)
    def _():
        hist_ref[0:POOL_HALO, :] = hist_ref[ROW_TILE:ROW_TILE + POOL_HALO, :]

    hist_ref[POOL_HALO:POOL_HALO + ROW_TILE, :] = u
    pos = si * ROW_TILE + lax.broadcasted_iota(jnp.int32, (ROW_TILE, 1), 0)
    for gi, win in enumerate(POOL_WINDOWS):
        lo = gi * POOL_GROUP
        ug = u[:, lo:lo + POOL_GROUP]
        wsum = ug
        for d in range(1, win):
            wsum = wsum + hist_ref[POOL_HALO - d:POOL_HALO - d + ROW_TILE, lo:lo + POOL_GROUP]
        cnt = jnp.minimum(pos + 1, win).astype(F32)
        pooled = wsum / cnt - ug
        y = jnp.dot(pooled.astype(BF16), wgrp_ref[gi], preferred_element_type=F32)
        o_ref[:, lo:lo + POOL_GROUP] = x[:, lo:lo + POOL_GROUP] + y * sc_ref[:, lo:lo + POOL_GROUP]


def _ffn_qkv_kernel(x_ref, g1_ref, wgu_ref, wd_ref, gm_ref, wqkv_ref, qg_ref, kg_ref, avg_ref,
                    x1_ref, q_ref, k_ref, v_ref, a_ref):
    x1 = _ffn_stage(x_ref[...], g1_ref, wgu_ref, wd_ref, a_ref)
    x1_ref[...] = x1
    _qkv_stage(x1, gm_ref, wqkv_ref, qg_ref, kg_ref, avg_ref, q_ref, k_ref, v_ref)


def _ffn_qkv(x, n1, w_gu, w_down, nm, w_qkv, q_gain, k_gain, avg, layer, j):
    t = x.shape[0]
    row = pl.BlockSpec((ROW_TILE, D_MODEL), lambda m: (m, 0))
    act = jax.ShapeDtypeStruct((t, D_MODEL), BF16)
    return pl.pallas_call(
        _ffn_qkv_kernel,
        grid=(t // ROW_TILE,),
        in_specs=[
            row,
            _layer_block((1, D_MODEL), layer),
            _layer_block((D_MODEL, 2 * D_FF), layer),
            _layer_block((D_FF, D_MODEL), layer),
            _layer_block((1, D_MODEL), layer),
            _layer_block((D_MODEL, 3 * D_MODEL), j),
            _layer_block((1, D_MODEL), j),
            _layer_block((1, D_MODEL), j),
            _resident((2 * LANES, LANES), lambda m: (0, 0)),
        ],
        out_specs=[row, row, row, row],
        out_shape=[jax.ShapeDtypeStruct((t, D_MODEL), F32), act, act, act],
        scratch_shapes=[pltpu.VMEM((ROW_TILE, D_FF), BF16)],
        compiler_params=_params("arbitrary"),
        name="ffn_qkv",
    )(x, n1, w_gu, w_down, nm, w_qkv, q_gain, k_gain, avg)


def _ffn_pool_kernel(x_ref, g1_ref, wgu_ref, wd_ref, gm_ref, win_ref, wgrp_ref, sc_ref,
                     o_ref, a_ref, hist_ref):
    x1 = _ffn_stage(x_ref[...], g1_ref, wgu_ref, wd_ref, a_ref)
    _pool_stage(x1, pl.program_id(1), gm_ref, win_ref, wgrp_ref, sc_ref, o_ref, hist_ref)


def _ffn_pool(x, n1, w_gu, w_down, nm, w_in, w_grp, scale, batch, seq, layer, j):
    t = x.shape[0]
    n_s = seq // ROW_TILE
    row = pl.BlockSpec((ROW_TILE, D_MODEL), lambda b, s: (b * n_s + s, 0))
    n_g = len(POOL_WINDOWS)
    return pl.pallas_call(
        _ffn_pool_kernel,
        grid=(batch, n_s),
        in_specs=[
            row,
            _layer_block((1, D_MODEL), layer),
            _layer_block((D_MODEL, 2 * D_FF), layer),
            _layer_block((D_FF, D_MODEL), layer),
            _layer_block((1, D_MODEL), layer),
            _layer_block((D_MODEL, D_MODEL), j),
            _layer_block((n_g, POOL_GROUP, POOL_GROUP), j),
            _layer_block((1, D_MODEL), j),
        ],
        out_specs=row,
        out_shape=jax.ShapeDtypeStruct((t, D_MODEL), F32),
        scratch_shapes=[pltpu.VMEM((ROW_TILE, D_FF), BF16),
                        pltpu.VMEM((ROW_TILE + POOL_HALO, D_MODEL), F32)],
        compiler_params=_params("arbitrary", "arbitrary"),
        name="ffn_pool",
    )(x, n1, w_gu, w_down, nm, w_in, w_grp, scale)


def _proj_ffn_ple_kernel(x_ref, o_ref, wo_ref, g2_ref, wgu_ref, wd_ref, p_ref, gp_ref, wg_ref,
                         wp_ref, y_ref, a_ref):
    x = x_ref[...] + jnp.dot(o_ref[...], wo_ref[...], preferred_element_type=F32)
    x = _ffn_stage(x, g2_ref, wgu_ref, wd_ref, a_ref)
    y_ref[...] = _ple_stage(x, p_ref, gp_ref, wg_ref, wp_ref)


def _ffn_ple_kernel(x_ref, g2_ref, wgu_ref, wd_ref, p_ref, gp_ref, wg_ref, wp_ref, y_ref, a_ref):
    x = _ffn_stage(x_ref[...], g2_ref, wgu_ref, wd_ref, a_ref)
    y_ref[...] = _ple_stage(x, p_ref, gp_ref, wg_ref, wp_ref)


def _ffn_ple(x, o, w_o, n2, w_gu, w_down, p, npl, w_gate, w_proj, layer, j):
    t = x.shape[0]
    row = pl.BlockSpec((ROW_TILE, D_MODEL), lambda m: (m, 0))
    tail_specs = [
        _layer_block((1, D_MODEL), layer),
        _layer_block((D_MODEL, 2 * D_FF), layer),
        _layer_block((D_FF, D_MODEL), layer),
        pl.BlockSpec((None, ROW_TILE, PLE_DIM), lambda m: (layer, m, 0)),
        _layer_block((1, D_MODEL), layer),
        _layer_block((D_MODEL, D_MODEL), layer),
        _layer_block((PLE_DIM, D_MODEL), layer),
    ]
    tail_args = (n2, w_gu, w_down, p, npl, w_gate, w_proj)
    if o is None:
        body, specs, args, name = _ffn_ple_kernel, [row] + tail_specs, (x,) + tail_args, "ffn_ple"
    else:
        body, name = _proj_ffn_ple_kernel, "proj_ffn_ple"
        specs = [row, row, _layer_block((D_MODEL, D_MODEL), j)] + tail_specs
        args = (x, o, w_o) + tail_args
    return pl.pallas_call(
        body,
        grid=(t // ROW_TILE,),
        in_specs=specs,
        out_specs=row,
        out_shape=jax.ShapeDtypeStruct((t, D_MODEL), F32),
        scratch_shapes=[pltpu.VMEM((ROW_TILE, D_FF), BF16)],
        compiler_params=_params("arbitrary"),
        name=name,
    )(*args)


def _tree_max(blocks):
    rows = [b[r:r + 8, c:c + LANES] for b in blocks
            for r in range(0, b.shape[0], 8) for c in range(0, b.shape[1], LANES)]
    while len(rows) > 1:
        rows = [jnp.maximum(a, b) for a, b in zip(rows[0::2], rows[1::2])] + rows[len(rows) & ~1:]
    return jnp.max(rows[0])


def _attn_kernel(q_ref, k_ref, v_ref, suf_ref, o_ref, carry_ref, acc_ref, live_ref):
    qb = pl.program_id(2)
    lane = lax.broadcasted_iota(jnp.int32, (ATT_TILE, LANES), 1)
    first_head = lane < HEAD_DIM
    row = lax.broadcasted_iota(jnp.int32, (ATT_TILE, 2 * ATT_TILE), 0)
    col = lax.broadcasted_iota(jnp.int32, (ATT_TILE, 2 * ATT_TILE), 1) & (ATT_TILE - 1)
    below_diag = col < row
    contract_last = (((1,), (1,)), ((), ()))

    def per_head(x):
        xz = jnp.zeros_like(x)
        return jnp.concatenate([jnp.where(first_head, x, xz), jnp.where(first_head, xz, x)], axis=0)

    def key_tile(kt):
        ks = pl.multiple_of(kt * ATT_TILE, ATT_TILE)
        return per_head(k_ref[pl.ds(ks, ATT_TILE), :]), per_head(v_ref[pl.ds(ks, ATT_TILE), :])

    def softplus(z):
        return jnp.maximum(z, 0.0) + jnp.log(1.0 + jnp.exp2(jnp.abs(z) * -LOG2E))

    def suffix_lhs(sp):
        hi, lo = _split_hi_lo(sp)
        return jnp.concatenate([hi, lo], axis=1)

    def row_totals(neg_sums):
        n = neg_sums.shape[0]
        return jnp.concatenate(
            [jnp.broadcast_to(neg_sums[:, 0:1], (n, ATT_TILE)),
             jnp.broadcast_to(neg_sums[:, ATT_TILE:ATT_TILE + 1], (n, ATT_TILE))], axis=1)

    def rows_used(i, j):
        return ATT_TILE if i - j < ATT_STATIC - 1 else ATT_TRIM

    def static_part(first_block):
        halo = 0 if first_block else ATT_STATIC - 1
        tiles = range(-halo, ATT_NSUB)
        users = {j: [i for i in range(ATT_NSUB) if 0 <= i - j < ATT_STATIC] for j in tiles}
        kv = {j: key_tile(qb * ATT_NSUB + j) for j in tiles}
        z = {}
        for j in tiles:
            r0 = users[j][0] * ATT_TILE
            n = sum(rows_used(i, j) for i in users[j])
            zz = lax.dot_general(q_ref[r0:r0 + n, :], kv[j][0], contract_last,
                                 preferred_element_type=F32)
            off = 0
            for i in users[j]:
                z[i, j] = zz[off:off + rows_used(i, j)]
                off += rows_used(i, j)
        acc = [None] * ATT_NSUB
        finals = []
        groups = [range(g, g + ATT_GROUP) for g in range(0, ATT_NSUB, ATT_GROUP)]

        def suffix_sums(group):
            pairs = [(i, j) for i in group for j in range(i, i - ATT_STATIC, -1) if j >= -halo]
            lhs, start, off = [], {}, 0
            for i, j in pairs:
                if i == j:
                    z[i, j] = jnp.where(below_diag, z[i, j], -MASKED_SCORE)
                lhs.append(suffix_lhs(softplus(z[i, j])))
                start[i, j] = off
                off += rows_used(i, j)
            return start, jnp.dot(jnp.concatenate(lhs, axis=0), suf_ref[...],
                                  preferred_element_type=F32)

        def weigh_values(group, start, neg_sums):
            g = group[0]
            wb = {}
            for i in group:
                carry = None
                for j in range(i, i - ATT_STATIC, -1):
                    if j < -halo:
                        continue
                    n = rows_used(i, j)
                    ns = neg_sums[start[i, j]:start[i, j] + n]
                    arg = z[i, j] + ns
                    if carry is not None:
                        arg = arg + carry[:n]
                    wb[i, j] = jnp.exp(arg).astype(BF16)
                    tot = row_totals(ns)
                    if carry is None:
                        carry = tot
                    elif n == ATT_TILE:
                        carry = carry + tot
                    else:
                        carry = jnp.concatenate([carry[:n] + tot, carry[n:]], axis=0)
                carry_ref[i] = carry
                finals.append(carry)
            for j in range(g + ATT_GROUP - 1, max(g - ATT_STATIC, -halo - 1), -1):
                mine = [i for i in group if (i, j) in wb]
                if not mine:
                    continue
                out = jnp.dot(jnp.concatenate([wb[i, j] for i in mine], axis=0), kv[j][1],
                              preferred_element_type=F32)
                off = 0
                for i in mine:
                    n = rows_used(i, j)
                    part = out[off:off + n]
                    off += n
                    if acc[i] is None:
                        acc[i] = part
                    elif n == ATT_TILE:
                        acc[i] = acc[i] + part
                    else:
                        acc[i] = jnp.concatenate([acc[i][:n] + part, acc[i][n:]], axis=0)
        pending = []
        for group in groups:
            pending.append((group,) + suffix_sums(group))
            if len(pending) > ATT_AHEAD:
                weigh_values(*pending.pop(0))
        for waiting in pending:
            weigh_values(*waiting)
        live_ref[0] = (_tree_max(finals) > EXP_ZERO_BELOW).astype(jnp.int32)
        for i in range(ATT_NSUB):
            acc_ref[i] = acc[i]
            o_ref[i * ATT_TILE:(i + 1) * ATT_TILE, :] = acc[i].astype(BF16)

    @pl.when(qb == 0)
    def _():
        static_part(True)

    @pl.when(qb > 0)
    def _():
        static_part(False)

    @pl.when(live_ref[0] > 0)
    def _():
        def finish(i, r0, r1, first_tile):
            rows = pl.ds(pl.multiple_of(i * ATT_TILE + r0, ATT_TRIM), r1 - r0)
            q = q_ref[rows, :]

            def cond(state):
                kt, live, _, _ = state
                return jnp.logical_and(kt >= 0, live)

            def body(state):
                kt, _, carry, acc = state
                kk, vv = key_tile(kt)
                z = lax.dot_general(q, kk, contract_last, preferred_element_type=F32)
                ns = jnp.dot(suffix_lhs(softplus(z)), suf_ref[...], preferred_element_type=F32)
                w = jnp.exp(z + ns + carry).astype(BF16)
                acc = acc + jnp.dot(w, vv, preferred_element_type=F32)
                carry = carry + row_totals(ns)
                return kt - 1, jnp.max(carry) > EXP_ZERO_BELOW, carry, acc

            carry = carry_ref[i, r0:r1, :]
            state = (first_tile, jnp.max(carry) > EXP_ZERO_BELOW, carry, acc_ref[i, r0:r1, :])
            acc = lax.while_loop(cond, body, state)[3]
            o_ref[rows, :] = acc.astype(BF16)

        def finish_sub_tile(i, _):
            newest_unseen = qb * ATT_NSUB + i - ATT_STATIC
            finish(i, 0, ATT_TRIM, newest_unseen)
            finish(i, ATT_TRIM, ATT_TILE, newest_unseen + 1)
            return 0

        lax.fori_loop(0, ATT_NSUB, finish_sub_tile, 0)


def _attention(q, k, v, suffix, batch, seq):
    t = q.shape[0]
    n_q = seq // ATT_BLOCK
    q_spec = pl.BlockSpec((ATT_BLOCK, LANES), lambda b, h, i: (b * n_q + i, h))
    kv_spec = pl.BlockSpec((seq, LANES), lambda b, h, i: (b, h))
    return pl.pallas_call(
        _attn_kernel,
        grid=(batch, N_HEADS // HEADS_PER_BLOCK, n_q),
        in_specs=[q_spec, kv_spec, kv_spec,
                  _resident((4 * ATT_TILE, 2 * ATT_TILE), lambda b, h, i: (0, 0))],
        out_specs=q_spec,
        out_shape=jax.ShapeDtypeStruct((t, D_MODEL), BF16),
        scratch_shapes=[pltpu.VMEM((ATT_NSUB, ATT_TILE, 2 * ATT_TILE), F32),
                        pltpu.VMEM((ATT_NSUB, ATT_TILE, LANES), F32),
                        pltpu.SMEM((1,), jnp.int32)],
        compiler_params=_params("arbitrary", "arbitrary", "arbitrary"),
        name="attn",
    )(q, k, v, suffix)


def _attention_constants():
    j = lax.broadcasted_iota(jnp.int32, (2 * ATT_TILE, 2 * ATT_TILE), 0)
    s = lax.broadcasted_iota(jnp.int32, (2 * ATT_TILE, 2 * ATT_TILE), 1)
    same_head = (j // ATT_TILE) == (s // ATT_TILE)
    half = jnp.where(jnp.logical_and(same_head, j >= s), -1.0, 0.0).astype(BF16)
    suffix = jnp.concatenate([half, half], axis=0)
    r = lax.broadcasted_iota(jnp.int32, (LANES, LANES), 0) // HEAD_DIM
    c = lax.broadcasted_iota(jnp.int32, (LANES, LANES), 1) // HEAD_DIM
    blk = jnp.where(r == c, 1.0 / HEAD_DIM, 0.0).astype(BF16)
    avg = jnp.concatenate([blk, blk], axis=0)
    return suffix, avg


def kernel(x, p, norm_ffn1, w_ffn1_gu, w_ffn1_down, norm_mix, w_qkv, q_norm, k_norm, w_o,
           w_pool_in, w_pool_grp, pool_scale, norm_ffn2, w_ffn2_gu, w_ffn2_down, norm_ple,
           w_ple_gate, w_ple_proj):
    batch, seq, _ = x.shape
    depth = norm_ffn1.shape[0]
    t = batch * seq
    bf = lambda w: w.astype(BF16)
    row3 = lambda g: g.reshape(g.shape[0], 1, g.shape[1])
    w1gu, w1d, w2gu, w2d = bf(w_ffn1_gu), bf(w_ffn1_down), bf(w_ffn2_gu), bf(w_ffn2_down)
    wqkv, wo, wpin, wpgrp = bf(w_qkv), bf(w_o), bf(w_pool_in), bf(w_pool_grp)
    wpg, wpp = bf(w_ple_gate), bf(w_ple_proj)
    n1, nm, n2, npl = row3(norm_ffn1), row3(norm_mix), row3(norm_ffn2), row3(norm_ple)
    qg = row3(jnp.tile(q_norm, (1, N_HEADS)))
    kg = row3(jnp.tile(k_norm, (1, N_HEADS)))
    psc = row3(pool_scale)
    suffix, avg = _attention_constants()
    p2 = p.reshape(depth, t, PLE_DIM)

    xs = x.reshape(t, D_MODEL)
    for i in range(depth):
        j = i // 2
        if i % 2 == 0:
            x1, q, k, v = _ffn_qkv(xs, n1, w1gu, w1d, nm, wqkv, qg, kg, avg, i, j)
            o = _attention(q, k, v, suffix, batch, seq)
        else:
            x1 = _ffn_pool(xs, n1, w1gu, w1d, nm, wpin, wpgrp, psc, batch, seq, i, j)
            o = None
        xs = _ffn_ple(x1, o, wo, n2, w2gu, w2d, p2, npl, wpg, wpp, i, j)
    return xs.reshape(batch, seq, D_MODEL)
```

```python
import jax
import jax.numpy as jnp
from jax import lax
from jax.experimental import pallas as pl
from jax.experimental.pallas import tpu as pltpu

D_MODEL = 1024
N_HEADS = 16
HEAD_DIM = D_MODEL // N_HEADS
D_FF = 2816
PLE_DIM = 256
POOL_WINDOWS = (2, 4, 8, 16)
POOL_GROUP = D_MODEL // len(POOL_WINDOWS)
EPS = 1e-6

LANES = 128
HEADS_PER_BLOCK = LANES // HEAD_DIM
FF_CHUNK = 256
N_FF_CHUNKS = D_FF // FF_CHUNK
ROW_TILE = 512
N_ROW_STEPS = 32
CAST_GU_ROWS = D_MODEL // N_ROW_STEPS
CAST_DOWN_ROWS = 2 * D_FF // N_ROW_STEPS
ATT_TILE = 128
ATT_NSUB = 16
ATT_BLOCK = ATT_TILE * ATT_NSUB
ATT_STATIC = 3
ATT_TRIM = 32
ATT_GROUP = 2
ATT_AHEAD = 2
MASKED_SCORE = 1e30
LOG2E = 1.4426950408889634
POOL_HALO = 16
EXP_ZERO_BELOW = -104.0
VMEM_LIMIT = 56 * 1024 * 1024

F32 = jnp.float32
BF16 = jnp.bfloat16


def _rms(x, g):
    ms = jnp.mean(x * x, axis=-1, keepdims=True)
    return x * lax.rsqrt(ms + EPS) * g


def _params(*sem):
    return pltpu.CompilerParams(dimension_semantics=sem, vmem_limit_bytes=VMEM_LIMIT)


def _resident(shape, index_map):
    return pl.BlockSpec(shape, index_map, pipeline_mode=pl.Buffered(1))


def _layer_block(shape, layer):
    zeros = (0,) * len(shape)
    return _resident((None,) + tuple(shape), lambda *_: (layer,) + zeros)


def _split_hi_lo(x):
    hi = lax.bitcast_convert_type(
        lax.bitcast_convert_type(x, jnp.uint32) & jnp.uint32(0xFFFF0000), F32)
    return hi.astype(BF16), (x - hi).astype(BF16)


def _ffn_stage(x, g_ref, wgu_ref, wd_ref, a_ref):
    h = _rms(x, g_ref[...]).astype(BF16)
    for c in range(N_FF_CHUNKS):
        lo = c * FF_CHUNK
        gate = jnp.dot(h, wgu_ref[:, lo:lo + FF_CHUNK], preferred_element_type=F32)
        up = jnp.dot(h, wgu_ref[:, D_FF + lo:D_FF + lo + FF_CHUNK], preferred_element_type=F32)
        a_ref[:, lo:lo + FF_CHUNK] = (gate * jax.nn.sigmoid(gate) * up).astype(BF16)
    return x + 0.5 * jnp.dot(a_ref[...], wd_ref[...], preferred_element_type=F32)


def _ple_stage(x, p_ref, g_ref, wg_ref, wp_ref):
    h = _rms(x, g_ref[...]).astype(BF16)
    gate = jax.nn.sigmoid(jnp.dot(h, wg_ref[...], preferred_element_type=F32))
    proj = jnp.dot(p_ref[...].astype(BF16), wp_ref[...], preferred_element_type=F32)
    return x + gate * proj


def _head_norm(y, gain, avg):
    cols = []
    for c in range(D_MODEL // LANES):
        blk = y[:, c * LANES:(c + 1) * LANES]
        hi, lo = _split_hi_lo(blk * blk)
        ms = jnp.dot(jnp.concatenate([hi, lo], axis=1), avg, preferred_element_type=F32)
        cols.append(blk * lax.rsqrt(ms + EPS))
    return jnp.concatenate(cols, axis=1) * gain


def _qkv_stage(x, g_ref, w_ref, qg_ref, kg_ref, avg_ref, q_ref, k_ref, v_ref):
    h = _rms(x, g_ref[...]).astype(BF16)
    avg = avg_ref[...]
    q = jnp.dot(h, w_ref[:, 0:D_MODEL], preferred_element_type=F32)
    q_ref[...] = (_head_norm(q, qg_ref[...], avg) * (HEAD_DIM ** -0.5)).astype(BF16)
    k = jnp.dot(h, w_ref[:, D_MODEL:2 * D_MODEL], preferred_element_type=F32)
    k_ref[...] = _head_norm(k, kg_ref[...], avg).astype(BF16)
    v = jnp.dot(h, w_ref[:, 2 * D_MODEL:3 * D_MODEL], preferred_element_type=F32)
    v_ref[...] = v.astype(BF16)


def _pool_stage(x, si, g_ref, win_ref, wgrp_ref, sc_ref, o_ref, hist_ref):
    h = _rms(x, g_ref[...]).astype(BF16)
    u = jnp.dot(h, win_ref[...], preferred_element_type=F32)

    @pl.when(si == 0)
    def _():
        hist_ref[0:POOL_HALO, :] = jnp.zeros((POOL_HALO, D_MODEL), F32)

    @pl.when(si > 0)
    def _():
        hist_ref[0:POOL_HALO, :] = hist_ref[ROW_TILE:ROW_TILE + POOL_HALO, :]

    hist_ref[POOL_HALO:POOL_HALO + ROW_TILE, :] = u
    pos = si * ROW_TILE + lax.broadcasted_iota(jnp.int32, (ROW_TILE, 1), 0)
    for gi, win in enumerate(POOL_WINDOWS):
        lo = gi * POOL_GROUP
        ug = u[:, lo:lo + POOL_GROUP]
        wsum = ug
        for d in range(1, win):
            wsum = wsum + hist_ref[POOL_HALO - d:POOL_HALO - d + ROW_TILE, lo:lo + POOL_GROUP]
        cnt = jnp.minimum(pos + 1, win).astype(F32)
        pooled = wsum / cnt - ug
        y = jnp.dot(pooled.astype(BF16), wgrp_ref[gi], preferred_element_type=F32)
        o_ref[:, lo:lo + POOL_GROUP] = x[:, lo:lo + POOL_GROUP] + y * sc_ref[:, lo:lo + POOL_GROUP]


def _with_weight_cast(body, n_in, n_out):
    def kernel(*refs):
        ins, src = refs[:n_in], refs[n_in:n_in + 2]
        outs = refs[n_in + 2:n_in + 2 + n_out]
        dst, scratch = refs[n_in + 2 + n_out:n_in + 4 + n_out], refs[n_in + 4 + n_out:]
        body(*ins, *outs, *scratch)
        for s, d in zip(src, dst):
            d[...] = s[...].astype(BF16)
    return kernel


def _per_token_call(body, name, grid, step, in_specs, args, out_specs, out_shape, scratch, cast):
    in_specs, out_specs, out_shape = list(in_specs), list(out_specs), list(out_shape)
    if cast is not None:
        w_gu, w_down, layer = cast
        body = _with_weight_cast(body, len(in_specs), len(out_specs))
        in_specs += [
            pl.BlockSpec((None, CAST_GU_ROWS, 2 * D_FF), lambda *g: (layer, step(*g), 0)),
            pl.BlockSpec((None, CAST_DOWN_ROWS, D_MODEL), lambda *g: (layer, step(*g) // 2, 0)),
        ]
        out_specs += [
            pl.BlockSpec((CAST_GU_ROWS, 2 * D_FF), lambda *g: (step(*g), 0)),
            pl.BlockSpec((CAST_DOWN_ROWS, D_MODEL), lambda *g: (step(*g) // 2, 0)),
        ]
        out_shape += [jax.ShapeDtypeStruct((D_MODEL, 2 * D_FF), BF16),
                      jax.ShapeDtypeStruct((D_FF, D_MODEL), BF16)]
        args = tuple(args) + (w_gu, w_down)
    return pl.pallas_call(
        body,
        grid=grid,
        in_specs=in_specs,
        out_specs=out_specs,
        out_shape=out_shape,
        scratch_shapes=scratch,
        compiler_params=_params(*(["arbitrary"] * len(grid))),
        name=name,
    )(*args)


def _ffn_weight_specs():
    return [_resident((D_MODEL, 2 * D_FF), lambda *_: (0, 0)),
            _resident((D_FF, D_MODEL), lambda *_: (0, 0))]


def _ffn_qkv_kernel(x_ref, g1_ref, wgu_ref, wd_ref, gm_ref, wqkv_ref, qg_ref, kg_ref, avg_ref,
                    x1_ref, q_ref, k_ref, v_ref, a_ref):
    x1 = _ffn_stage(x_ref[...], g1_ref, wgu_ref, wd_ref, a_ref)
    x1_ref[...] = x1
    _qkv_stage(x1, gm_ref, wqkv_ref, qg_ref, kg_ref, avg_ref, q_ref, k_ref, v_ref)


def _ffn_qkv(x, n1, ffn_w, nm, w_qkv, q_gain, k_gain, avg, layer, j, cast):
    t = x.shape[0]
    row = pl.BlockSpec((ROW_TILE, D_MODEL), lambda m: (m, 0))
    act = jax.ShapeDtypeStruct((t, D_MODEL), BF16)
    return _per_token_call(
        _ffn_qkv_kernel, "ffn_qkv", (t // ROW_TILE,), lambda m: m,
        [row, _layer_block((1, D_MODEL), layer)] + _ffn_weight_specs() + [
            _layer_block((1, D_MODEL), layer),
            _layer_block((D_MODEL, 3 * D_MODEL), j),
            _layer_block((1, D_MODEL), j),
            _layer_block((1, D_MODEL), j),
            _resident((2 * LANES, LANES), lambda m: (0, 0)),
        ],
        (x, n1) + tuple(ffn_w) + (nm, w_qkv, q_gain, k_gain, avg),
        [row, row, row, row],
        [jax.ShapeDtypeStruct((t, D_MODEL), F32), act, act, act],
        [pltpu.VMEM((ROW_TILE, D_FF), BF16)],
        cast)


def _ffn_pool_kernel(x_ref, g1_ref, wgu_ref, wd_ref, gm_ref, win_ref, wgrp_ref, sc_ref,
                     o_ref, a_ref, hist_ref):
    x1 = _ffn_stage(x_ref[...], g1_ref, wgu_ref, wd_ref, a_ref)
    _pool_stage(x1, pl.program_id(1), gm_ref, win_ref, wgrp_ref, sc_ref, o_ref, hist_ref)


def _ffn_pool(x, n1, ffn_w, nm, w_in, w_grp, scale, batch, seq, layer, j, cast):
    t = x.shape[0]
    n_s = seq // ROW_TILE
    row = pl.BlockSpec((ROW_TILE, D_MODEL), lambda b, s: (b * n_s + s, 0))
    n_g = len(POOL_WINDOWS)
    return _per_token_call(
        _ffn_pool_kernel, "ffn_pool", (batch, n_s), lambda b, s: b * n_s + s,
        [row, _layer_block((1, D_MODEL), layer)] + _ffn_weight_specs() + [
            _layer_block((1, D_MODEL), layer),
            _layer_block((D_MODEL, D_MODEL), j),
            _layer_block((n_g, POOL_GROUP, POOL_GROUP), j),
            _layer_block((1, D_MODEL), j),
        ],
        (x, n1) + tuple(ffn_w) + (nm, w_in, w_grp, scale),
        [row],
        [jax.ShapeDtypeStruct((t, D_MODEL), F32)],
        [pltpu.VMEM((ROW_TILE, D_FF), BF16),
         pltpu.VMEM((ROW_TILE + POOL_HALO, D_MODEL), F32)],
        cast)


def _proj_ffn_ple_kernel(x_ref, o_ref, wo_ref, g2_ref, wgu_ref, wd_ref, p_ref, gp_ref, wg_ref,
                         wp_ref, y_ref, a_ref):
    x = x_ref[...] + jnp.dot(o_ref[...], wo_ref[...], preferred_element_type=F32)
    x = _ffn_stage(x, g2_ref, wgu_ref, wd_ref, a_ref)
    y_ref[...] = _ple_stage(x, p_ref, gp_ref, wg_ref, wp_ref)


def _ffn_ple_kernel(x_ref, g2_ref, wgu_ref, wd_ref, p_ref, gp_ref, wg_ref, wp_ref, y_ref, a_ref):
    x = _ffn_stage(x_ref[...], g2_ref, wgu_ref, wd_ref, a_ref)
    y_ref[...] = _ple_stage(x, p_ref, gp_ref, wg_ref, wp_ref)


def _ffn_ple(x, o, w_o, n2, ffn_w, p, npl, w_gate, w_proj, layer, j, cast):
    t = x.shape[0]
    row = pl.BlockSpec((ROW_TILE, D_MODEL), lambda m: (m, 0))
    tail_specs = [_layer_block((1, D_MODEL), layer)] + _ffn_weight_specs() + [
        pl.BlockSpec((None, ROW_TILE, PLE_DIM), lambda m: (layer, m, 0)),
        _layer_block((1, D_MODEL), layer),
        _layer_block((D_MODEL, D_MODEL), layer),
        _layer_block((PLE_DIM, D_MODEL), layer),
    ]
    tail_args = (n2,) + tuple(ffn_w) + (p, npl, w_gate, w_proj)
    if o is None:
        body, specs, args, name = _ffn_ple_kernel, [row] + tail_specs, (x,) + tail_args, "ffn_ple"
    else:
        body, name = _proj_ffn_ple_kernel, "proj_ffn_ple"
        specs = [row, row, _layer_block((D_MODEL, D_MODEL), j)] + tail_specs
        args = (x, o, w_o) + tail_args
    return _per_token_call(
        body, name, (t // ROW_TILE,), lambda m: m, specs, args,
        [row], [jax.ShapeDtypeStruct((t, D_MODEL), F32)],
        [pltpu.VMEM((ROW_TILE, D_FF), BF16)], cast)


def _tree_max(blocks):
    rows = [b[r:r + 8, c:c + LANES] for b in blocks
            for r in range(0, b.shape[0], 8) for c in range(0, b.shape[1], LANES)]
    while len(rows) > 1:
        rows = [jnp.maximum(a, b) for a, b in zip(rows[0::2], rows[1::2])] + rows[len(rows) & ~1:]
    return jnp.max(rows[0])


def _attn_kernel(q_ref, k_ref, v_ref, suf_ref, o_ref, carry_ref, acc_ref, live_ref):
    qb = pl.program_id(2)
    lane = lax.broadcasted_iota(jnp.int32, (ATT_TILE, LANES), 1)
    first_head = lane < HEAD_DIM
    row = lax.broadcasted_iota(jnp.int32, (ATT_TILE, 2 * ATT_TILE), 0)
    col = lax.broadcasted_iota(jnp.int32, (ATT_TILE, 2 * ATT_TILE), 1) & (ATT_TILE - 1)
    below_diag = col < row
    contract_last = (((1,), (1,)), ((), ()))

    def per_head(x):
        xz = jnp.zeros_like(x)
        return jnp.concatenate([jnp.where(first_head, x, xz), jnp.where(first_head, xz, x)], axis=0)

    def key_tile(kt):
        ks = pl.multiple_of(kt * ATT_TILE, ATT_TILE)
        return per_head(k_ref[pl.ds(ks, ATT_TILE), :]), per_head(v_ref[pl.ds(ks, ATT_TILE), :])

    def softplus(z):
        return jnp.maximum(z, 0.0) + jnp.log(1.0 + jnp.exp2(jnp.abs(z) * -LOG2E))

    def suffix_lhs(sp):
        hi, lo = _split_hi_lo(sp)
        return jnp.concatenate([hi, lo], axis=1)

    def row_totals(neg_sums):
        n = neg_sums.shape[0]
        return jnp.concatenate(
            [jnp.broadcast_to(neg_sums[:, 0:1], (n, ATT_TILE)),
             jnp.broadcast_to(neg_sums[:, ATT_TILE:ATT_TILE + 1], (n, ATT_TILE))], axis=1)

    def rows_used(i, j):
        return ATT_TILE if i - j < ATT_STATIC - 1 else ATT_TRIM

    def static_part(first_block):
        halo = 0 if first_block else ATT_STATIC - 1
        tiles = range(-halo, ATT_NSUB)
        users = {j: [i for i in range(ATT_NSUB) if 0 <= i - j < ATT_STATIC] for j in tiles}
        kv = {j: key_tile(qb * ATT_NSUB + j) for j in tiles}
        z = {}
        for j in tiles:
            r0 = users[j][0] * ATT_TILE
            n = sum(rows_used(i, j) for i in users[j])
            zz = lax.dot_general(q_ref[r0:r0 + n, :], kv[j][0], contract_last,
                                 preferred_element_type=F32)
            off = 0
            for i in users[j]:
                z[i, j] = zz[off:off + rows_used(i, j)]
                off += rows_used(i, j)
        acc = [None] * ATT_NSUB
        finals = []
        groups = [range(g, g + ATT_GROUP) for g in range(0, ATT_NSUB, ATT_GROUP)]

        def suffix_sums(group):
            pairs = [(i, j) for i in group for j in range(i, i - ATT_STATIC, -1) if j >= -halo]
            lhs, start, off = [], {}, 0
            for i, j in pairs:
                if i == j:
                    z[i, j] = jnp.where(below_diag, z[i, j], -MASKED_SCORE)
                lhs.append(suffix_lhs(softplus(z[i, j])))
                start[i, j] = off
                off += rows_used(i, j)
            return start, jnp.dot(jnp.concatenate(lhs, axis=0), suf_ref[...],
                                  preferred_element_type=F32)

        def weigh_values(group, start, neg_sums):
            g = group[0]
            wb = {}
            for i in group:
                carry = None
                for j in range(i, i - ATT_STATIC, -1):
                    if j < -halo:
                        continue
                    n = rows_used(i, j)
                    ns = neg_sums[start[i, j]:start[i, j] + n]
                    arg = z[i, j] + ns
                    if carry is not None:
                        arg = arg + carry[:n]
                    wb[i, j] = jnp.exp(arg).astype(BF16)
                    tot = row_totals(ns)
                    if carry is None:
                        carry = tot
                    elif n == ATT_TILE:
                        carry = carry + tot
                    else:
                        carry = jnp.concatenate([carry[:n] + tot, carry[n:]], axis=0)
                carry_ref[i] = carry
                finals.append(carry)
            for j in range(g + ATT_GROUP - 1, max(g - ATT_STATIC, -halo - 1), -1):
                mine = [i for i in group if (i, j) in wb]
                if not mine:
                    continue
                out = jnp.dot(jnp.concatenate([wb[i, j] for i in mine], axis=0), kv[j][1],
                              preferred_element_type=F32)
                off = 0
                for i in mine:
                    n = rows_used(i, j)
                    part = out[off:off + n]
                    off += n
                    if acc[i] is None:
                        acc[i] = part
                    elif n == ATT_TILE:
                        acc[i] = acc[i] + part
                    else:
                        acc[i] = jnp.concatenate([acc[i][:n] + part, acc[i][n:]], axis=0)
        pending = []
        for group in groups:
            pending.append((group,) + suffix_sums(group))
            if len(pending) > ATT_AHEAD:
                weigh_values(*pending.pop(0))
        for waiting in pending:
            weigh_values(*waiting)
        live_ref[0] = (_tree_max(finals) > EXP_ZERO_BELOW).astype(jnp.int32)
        for i in range(ATT_NSUB):
            acc_ref[i] = acc[i]
            o_ref[i * ATT_TILE:(i + 1) * ATT_TILE, :] = acc[i].astype(BF16)

    @pl.when(qb == 0)
    def _():
        static_part(True)

    @pl.when(qb > 0)
    def _():
        static_part(False)

    @pl.when(live_ref[0] > 0)
    def _():
        def finish(i, r0, r1, first_tile):
            rows = pl.ds(pl.multiple_of(i * ATT_TILE + r0, ATT_TRIM), r1 - r0)
            q = q_ref[rows, :]

            def cond(state):
                kt, live, _, _ = state
                return jnp.logical_and(kt >= 0, live)

            def body(state):
                kt, _, carry, acc = state
                kk, vv = key_tile(kt)
                z = lax.dot_general(q, kk, contract_last, preferred_element_type=F32)
                ns = jnp.dot(suffix_lhs(softplus(z)), suf_ref[...], preferred_element_type=F32)
                w = jnp.exp(z + ns + carry).astype(BF16)
                acc = acc + jnp.dot(w, vv, preferred_element_type=F32)
                carry = carry + row_totals(ns)
                return kt - 1, jnp.max(carry) > EXP_ZERO_BELOW, carry, acc

            carry = carry_ref[i, r0:r1, :]
            state = (first_tile, jnp.max(carry) > EXP_ZERO_BELOW, carry, acc_ref[i, r0:r1, :])
            acc = lax.while_loop(cond, body, state)[3]
            o_ref[rows, :] = acc.astype(BF16)

        def finish_sub_tile(i, _):
            newest_unseen = qb * ATT_NSUB + i - ATT_STATIC
            finish(i, 0, ATT_TRIM, newest_unseen)
            finish(i, ATT_TRIM, ATT_TILE, newest_unseen + 1)
            return 0

        lax.fori_loop(0, ATT_NSUB, finish_sub_tile, 0)


def _attention(q, k, v, suffix, batch, seq):
    t = q.shape[0]
    n_q = seq // ATT_BLOCK
    q_spec = pl.BlockSpec((ATT_BLOCK, LANES), lambda b, h, i: (b * n_q + i, h))
    kv_spec = pl.BlockSpec((seq, LANES), lambda b, h, i: (b, h))
    return pl.pallas_call(
        _attn_kernel,
        grid=(batch, N_HEADS // HEADS_PER_BLOCK, n_q),
        in_specs=[q_spec, kv_spec, kv_spec,
                  _resident((4 * ATT_TILE, 2 * ATT_TILE), lambda b, h, i: (0, 0))],
        out_specs=q_spec,
        out_shape=jax.ShapeDtypeStruct((t, D_MODEL), BF16),
        scratch_shapes=[pltpu.VMEM((ATT_NSUB, ATT_TILE, 2 * ATT_TILE), F32),
                        pltpu.VMEM((ATT_NSUB, ATT_TILE, LANES), F32),
                        pltpu.SMEM((1,), jnp.int32)],
        compiler_params=_params("arbitrary", "arbitrary", "arbitrary"),
        name="attn",
    )(q, k, v, suffix)


def _attention_constants():
    j = lax.broadcasted_iota(jnp.int32, (2 * ATT_TILE, 2 * ATT_TILE), 0)
    s = lax.broadcasted_iota(jnp.int32, (2 * ATT_TILE, 2 * ATT_TILE), 1)
    same_head = (j // ATT_TILE) == (s // ATT_TILE)
    half = jnp.where(jnp.logical_and(same_head, j >= s), -1.0, 0.0).astype(BF16)
    suffix = jnp.concatenate([half, half], axis=0)
    r = lax.broadcasted_iota(jnp.int32, (LANES, LANES), 0) // HEAD_DIM
    c = lax.broadcasted_iota(jnp.int32, (LANES, LANES), 1) // HEAD_DIM
    blk = jnp.where(r == c, 1.0 / HEAD_DIM, 0.0).astype(BF16)
    avg = jnp.concatenate([blk, blk], axis=0)
    return suffix, avg


def kernel(x, p, norm_ffn1, w_ffn1_gu, w_ffn1_down, norm_mix, w_qkv, q_norm, k_norm, w_o,
           w_pool_in, w_pool_grp, pool_scale, norm_ffn2, w_ffn2_gu, w_ffn2_down, norm_ple,
           w_ple_gate, w_ple_proj):
    batch, seq, _ = x.shape
    depth = norm_ffn1.shape[0]
    t = batch * seq
    bf = lambda w: w.astype(BF16)
    row3 = lambda g: g.reshape(g.shape[0], 1, g.shape[1])
    assert t == N_ROW_STEPS * ROW_TILE
    wqkv, wo, wpin, wpgrp = bf(w_qkv), bf(w_o), bf(w_pool_in), bf(w_pool_grp)
    wpg, wpp = bf(w_ple_gate), bf(w_ple_proj)
    n1, nm, n2, npl = row3(norm_ffn1), row3(norm_mix), row3(norm_ffn2), row3(norm_ple)
    qg = row3(jnp.tile(q_norm, (1, N_HEADS)))
    kg = row3(jnp.tile(k_norm, (1, N_HEADS)))
    psc = row3(pool_scale)
    suffix, avg = _attention_constants()
    p2 = p.reshape(depth, t, PLE_DIM)

    xs = x.reshape(t, D_MODEL)
    ffn_w = (bf(w_ffn1_gu[0]), bf(w_ffn1_down[0]))
    for i in range(depth):
        j = i // 2
        cast2 = (w_ffn2_gu, w_ffn2_down, i)
        if i % 2 == 0:
            x1, q, k, v, *ffn_w = _ffn_qkv(xs, n1, ffn_w, nm, wqkv, qg, kg, avg, i, j, cast2)
            o = _attention(q, k, v, suffix, batch, seq)
        else:
            x1, *ffn_w = _ffn_pool(xs, n1, ffn_w, nm, wpin, wpgrp, psc, batch, seq, i, j, cast2)
            o = None
        cast1 = (w_ffn1_gu, w_ffn1_down, i + 1) if i + 1 < depth else None
        xs, *ffn_w = _ffn_ple(x1, o, wo, n2, ffn_w, p2, npl, wpg, wpp, i, j, cast1)
    return xs.reshape(batch, seq, D_MODEL)
```

```python
import jax
import jax.numpy as jnp
from jax import lax
from jax.experimental import pallas as pl
from jax.experimental.pallas import tpu as pltpu

D_MODEL = 1024
N_HEADS = 16
HEAD_DIM = D_MODEL // N_HEADS
D_FF = 2816
PLE_DIM = 256
POOL_WINDOWS = (2, 4, 8, 16)
POOL_GROUP = D_MODEL // len(POOL_WINDOWS)
EPS = 1e-6

LANES = 128
HEADS_PER_BLOCK = LANES // HEAD_DIM
MXU_WIDTH = 256
FF_CHUNK = MXU_WIDTH
N_FF_CHUNKS = D_FF // FF_CHUNK
ROW_TILE = 512
N_ROW_STEPS = 32
CAST_GU_ROWS = D_MODEL // N_ROW_STEPS
CAST_DOWN_ROWS = 2 * D_FF // N_ROW_STEPS
ATT_TILE = 128
ATT_NSUB = 8
ATT_BLOCK = ATT_TILE * ATT_NSUB
ATT_STATIC = 3
ATT_TRIM = 32
ATT_GROUP = 2
ATT_AHEAD = 2
MASKED_SCORE = 1e30
LOG2E = 1.4426950408889634
POOL_HALO = 32
assert all(w == 2 << i for i, w in enumerate(POOL_WINDOWS)) and POOL_HALO == 8 * len(POOL_WINDOWS)
POOL_LEVEL_COLS = sum(range(len(POOL_WINDOWS))) * POOL_GROUP
EXP_ZERO_BELOW = -104.0
VMEM_LIMIT = 56 * 1024 * 1024

F32 = jnp.float32
BF16 = jnp.bfloat16


def _rms(x, g):
    ms = jnp.mean(x * x, axis=-1, keepdims=True)
    return x * lax.rsqrt(ms + EPS) * g


def _params(*sem):
    return pltpu.CompilerParams(dimension_semantics=sem, vmem_limit_bytes=VMEM_LIMIT)


def _resident(shape, index_map):
    return pl.BlockSpec(shape, index_map, pipeline_mode=pl.Buffered(1))


def _layer_block(shape, layer):
    zeros = (0,) * len(shape)
    return _resident((None,) + tuple(shape), lambda *_: (layer,) + zeros)


def _split_hi_lo(x):
    hi = lax.bitcast_convert_type(
        lax.bitcast_convert_type(x, jnp.uint32) & jnp.uint32(0xFFFF0000), F32)
    return hi.astype(BF16), (x - hi).astype(BF16)


def _ffn_stage(x, g_ref, wgu_ref, wd_ref, a_ref):
    h = _rms(x, g_ref[...]).astype(BF16)
    for c in range(N_FF_CHUNKS):
        lo = c * FF_CHUNK
        gate = jnp.dot(h, wgu_ref[:, lo:lo + FF_CHUNK], preferred_element_type=F32)
        up = jnp.dot(h, wgu_ref[:, D_FF + lo:D_FF + lo + FF_CHUNK], preferred_element_type=F32)
        a_ref[:, lo:lo + FF_CHUNK] = (gate * jax.nn.sigmoid(gate) * up).astype(BF16)
    return x + 0.5 * jnp.dot(a_ref[...], wd_ref[...], preferred_element_type=F32)


def _ple_stage(x, p_ref, g_ref, wg_ref, wp_ref):
    h = _rms(x, g_ref[...]).astype(BF16)
    gate = jax.nn.sigmoid(jnp.dot(h, wg_ref[...], preferred_element_type=F32))
    proj = jnp.dot(p_ref[...].astype(BF16), wp_ref[...], preferred_element_type=F32)
    return x + gate * proj


def _head_norm(y, gain, avg):
    width = avg.shape[0]
    cols = []
    for c in range(D_MODEL // width):
        blk = y[:, c * width:(c + 1) * width]
        ms = jnp.dot((blk * blk).astype(BF16), avg, preferred_element_type=F32)
        cols.append(blk * lax.rsqrt(ms + EPS))
    return jnp.concatenate(cols, axis=1) * gain


def _qkv_stage(x, g_ref, w_ref, qg_ref, kg_ref, avg_ref, q_ref, k_ref, v_ref):
    h = _rms(x, g_ref[...]).astype(BF16)
    avg = avg_ref[...]
    q = jnp.dot(h, w_ref[:, 0:D_MODEL], preferred_element_type=F32)
    q_ref[...] = (_head_norm(q, qg_ref[...], avg) * (HEAD_DIM ** -0.5)).astype(BF16)
    k = jnp.dot(h, w_ref[:, D_MODEL:2 * D_MODEL], preferred_element_type=F32)
    k_ref[...] = _head_norm(k, kg_ref[...], avg).astype(BF16)
    v = jnp.dot(h, w_ref[:, 2 * D_MODEL:3 * D_MODEL], preferred_element_type=F32)
    v_ref[...] = v.astype(BF16)


def _pool_stage(x, si, g_ref, win_ref, wgrp_ref, sc_ref, o_ref, hist_ref, lvl_ref):
    h = _rms(x, g_ref[...]).astype(BF16)
    u = jnp.dot(h, win_ref[...], preferred_element_type=F32)

    @pl.when(si == 0)
    def _():
        hist_ref[0:POOL_HALO, :] = jnp.zeros((POOL_HALO, D_MODEL), F32)

    @pl.when(si > 0)
    def _():
        hist_ref[0:POOL_HALO, :] = hist_ref[ROW_TILE:ROW_TILE + POOL_HALO, :]

    hist_ref[POOL_HALO:POOL_HALO + ROW_TILE, :] = u
    n_rows = POOL_HALO + ROW_TILE
    n_groups = len(POOL_WINDOWS)
    pos = si * ROW_TILE + lax.broadcasted_iota(jnp.int32, (ROW_TILE, 1), 0)

    def lvl_col(k, gi):
        return sum(n_groups - m for m in range(1, k)) * POOL_GROUP + (gi - k) * POOL_GROUP

    def shifted_sum(k, gi, r0):
        shift = 1 << (k - 1)
        if k == 1:
            src, c = hist_ref, gi * POOL_GROUP
        else:
            src, c = lvl_ref, lvl_col(k - 1, gi)
        return (src[r0:n_rows, c:c + POOL_GROUP]
                + src[r0 - shift:n_rows - shift, c:c + POOL_GROUP])

    for k in range(1, n_groups + 1):
        for gi in range(k, n_groups):
            c = lvl_col(k, gi)
            lvl_ref[8 * k:n_rows, c:c + POOL_GROUP] = shifted_sum(k, gi, 8 * k)
        gi, win = k - 1, POOL_WINDOWS[k - 1]
        lo = gi * POOL_GROUP
        wsum = shifted_sum(k, gi, POOL_HALO)
        cnt = jnp.minimum(pos + 1, win).astype(F32)
        pooled = wsum / cnt - u[:, lo:lo + POOL_GROUP]
        y = jnp.dot(pooled.astype(BF16), wgrp_ref[gi], preferred_element_type=F32)
        o_ref[:, lo:lo + POOL_GROUP] = x[:, lo:lo + POOL_GROUP] + y * sc_ref[:, lo:lo + POOL_GROUP]


def _with_weight_cast(body, n_in, n_out):
    def kernel(*refs):
        ins, src = refs[:n_in], refs[n_in:n_in + 2]
        outs = refs[n_in + 2:n_in + 2 + n_out]
        dst, scratch = refs[n_in + 2 + n_out:n_in + 4 + n_out], refs[n_in + 4 + n_out:]
        body(*ins, *outs, *scratch)
        for s, d in zip(src, dst):
            d[...] = s[...].astype(BF16)
    return kernel


def _per_token_call(body, name, grid, step, in_specs, args, out_specs, out_shape, scratch, cast):
    in_specs, out_specs, out_shape = list(in_specs), list(out_specs), list(out_shape)
    if cast is not None:
        w_gu, w_down, layer = cast
        body = _with_weight_cast(body, len(in_specs), len(out_specs))
        in_specs += [
            pl.BlockSpec((None, CAST_GU_ROWS, 2 * D_FF), lambda *g: (layer, step(*g), 0)),
            pl.BlockSpec((None, CAST_DOWN_ROWS, D_MODEL), lambda *g: (layer, step(*g) // 2, 0)),
        ]
        out_specs += [
            pl.BlockSpec((CAST_GU_ROWS, 2 * D_FF), lambda *g: (step(*g), 0)),
            pl.BlockSpec((CAST_DOWN_ROWS, D_MODEL), lambda *g: (step(*g) // 2, 0)),
        ]
        out_shape += [jax.ShapeDtypeStruct((D_MODEL, 2 * D_FF), BF16),
                      jax.ShapeDtypeStruct((D_FF, D_MODEL), BF16)]
        args = tuple(args) + (w_gu, w_down)
    return pl.pallas_call(
        body,
        grid=grid,
        in_specs=in_specs,
        out_specs=out_specs,
        out_shape=out_shape,
        scratch_shapes=scratch,
        compiler_params=_params(*(["arbitrary"] * len(grid))),
        name=name,
    )(*args)


def _ffn_weight_specs():
    return [_resident((D_MODEL, 2 * D_FF), lambda *_: (0, 0)),
            _resident((D_FF, D_MODEL), lambda *_: (0, 0))]


def _ffn_qkv_kernel(x_ref, g1_ref, wgu_ref, wd_ref, gm_ref, wqkv_ref, qg_ref, kg_ref, avg_ref,
                    x1_ref, q_ref, k_ref, v_ref, a_ref):
    x1 = _ffn_stage(x_ref[...], g1_ref, wgu_ref, wd_ref, a_ref)
    x1_ref[...] = x1
    _qkv_stage(x1, gm_ref, wqkv_ref, qg_ref, kg_ref, avg_ref, q_ref, k_ref, v_ref)


def _ffn_qkv(x, n1, ffn_w, nm, w_qkv, q_gain, k_gain, avg, layer, j, cast):
    t = x.shape[0]
    row = pl.BlockSpec((ROW_TILE, D_MODEL), lambda m: (m, 0))
    act = jax.ShapeDtypeStruct((t, D_MODEL), BF16)
    return _per_token_call(
        _ffn_qkv_kernel, "ffn_qkv", (t // ROW_TILE,), lambda m: m,
        [row, _layer_block((1, D_MODEL), layer)] + _ffn_weight_specs() + [
            _layer_block((1, D_MODEL), layer),
            _layer_block((D_MODEL, 3 * D_MODEL), j),
            _layer_block((1, D_MODEL), j),
            _layer_block((1, D_MODEL), j),
            _resident((MXU_WIDTH, MXU_WIDTH), lambda m: (0, 0)),
        ],
        (x, n1) + tuple(ffn_w) + (nm, w_qkv, q_gain, k_gain, avg),
        [row, row, row, row],
        [jax.ShapeDtypeStruct((t, D_MODEL), F32), act, act, act],
        [pltpu.VMEM((ROW_TILE, D_FF), BF16)],
        cast)


def _ffn_pool_kernel(x_ref, g1_ref, wgu_ref, wd_ref, gm_ref, win_ref, wgrp_ref, sc_ref,
                     o_ref, a_ref, hist_ref, lvl_ref):
    x1 = _ffn_stage(x_ref[...], g1_ref, wgu_ref, wd_ref, a_ref)
    _pool_stage(x1, pl.program_id(1), gm_ref, win_ref, wgrp_ref, sc_ref, o_ref, hist_ref, lvl_ref)


def _ffn_pool(x, n1, ffn_w, nm, w_in, w_grp, scale, batch, seq, layer, j, cast):
    t = x.shape[0]
    n_s = seq // ROW_TILE
    row = pl.BlockSpec((ROW_TILE, D_MODEL), lambda b, s: (b * n_s + s, 0))
    n_g = len(POOL_WINDOWS)
    return _per_token_call(
        _ffn_pool_kernel, "ffn_pool", (batch, n_s), lambda b, s: b * n_s + s,
        [row, _layer_block((1, D_MODEL), layer)] + _ffn_weight_specs() + [
            _layer_block((1, D_MODEL), layer),
            _layer_block((D_MODEL, D_MODEL), j),
            _layer_block((n_g, POOL_GROUP, POOL_GROUP), j),
            _layer_block((1, D_MODEL), j),
        ],
        (x, n1) + tuple(ffn_w) + (nm, w_in, w_grp, scale),
        [row],
        [jax.ShapeDtypeStruct((t, D_MODEL), F32)],
        [pltpu.VMEM((ROW_TILE, D_FF), BF16),
         pltpu.VMEM((ROW_TILE + POOL_HALO, D_MODEL), F32),
         pltpu.VMEM((ROW_TILE + POOL_HALO, POOL_LEVEL_COLS), F32)],
        cast)


def _proj_ffn_ple_kernel(x_ref, o_ref, wo_ref, g2_ref, wgu_ref, wd_ref, p_ref, gp_ref, wg_ref,
                         wp_ref, y_ref, a_ref):
    x = x_ref[...] + jnp.dot(o_ref[...], wo_ref[...], preferred_element_type=F32)
    x = _ffn_stage(x, g2_ref, wgu_ref, wd_ref, a_ref)
    y_ref[...] = _ple_stage(x, p_ref, gp_ref, wg_ref, wp_ref)


def _ffn_ple_kernel(x_ref, g2_ref, wgu_ref, wd_ref, p_ref, gp_ref, wg_ref, wp_ref, y_ref, a_ref):
    x = _ffn_stage(x_ref[...], g2_ref, wgu_ref, wd_ref, a_ref)
    y_ref[...] = _ple_stage(x, p_ref, gp_ref, wg_ref, wp_ref)


def _ffn_ple(x, o, w_o, n2, ffn_w, p, npl, w_gate, w_proj, layer, j, cast):
    t = x.shape[0]
    row = pl.BlockSpec((ROW_TILE, D_MODEL), lambda m: (m, 0))
    tail_specs = [_layer_block((1, D_MODEL), layer)] + _ffn_weight_specs() + [
        pl.BlockSpec((None, ROW_TILE, PLE_DIM), lambda m: (layer, m, 0)),
        _layer_block((1, D_MODEL), layer),
        _layer_block((D_MODEL, D_MODEL), layer),
        _layer_block((PLE_DIM, D_MODEL), layer),
    ]
    tail_args = (n2,) + tuple(ffn_w) + (p, npl, w_gate, w_proj)
    if o is None:
        body, specs, args, name = _ffn_ple_kernel, [row] + tail_specs, (x,) + tail_args, "ffn_ple"
    else:
        body, name = _proj_ffn_ple_kernel, "proj_ffn_ple"
        specs = [row, row, _layer_block((D_MODEL, D_MODEL), j)] + tail_specs
        args = (x, o, w_o) + tail_args
    return _per_token_call(
        body, name, (t // ROW_TILE,), lambda m: m, specs, args,
        [row], [jax.ShapeDtypeStruct((t, D_MODEL), F32)],
        [pltpu.VMEM((ROW_TILE, D_FF), BF16)], cast)


def _tree_max(blocks):
    rows = [b[r:r + 8, c:c + LANES] for b in blocks
            for r in range(0, b.shape[0], 8) for c in range(0, b.shape[1], LANES)]
    while len(rows) > 1:
        rows = [jnp.maximum(a, b) for a, b in zip(rows[0::2], rows[1::2])] + rows[len(rows) & ~1:]
    return jnp.max(rows[0])


def _attn_kernel(q_ref, k_ref, v_ref, suf_ref, o_ref, carry_ref, acc_ref, live_ref):
    qb = pl.program_id(2)
    lane = lax.broadcasted_iota(jnp.int32, (ATT_TILE, LANES), 1)
    first_head = lane < HEAD_DIM
    row = lax.broadcasted_iota(jnp.int32, (ATT_TILE, 2 * ATT_TILE), 0)
    col = lax.broadcasted_iota(jnp.int32, (ATT_TILE, 2 * ATT_TILE), 1) & (ATT_TILE - 1)
    below_diag = col < row
    contract_last = (((1,), (1,)), ((), ()))

    def per_head(x):
        xz = jnp.zeros_like(x)
        return jnp.concatenate([jnp.where(first_head, x, xz), jnp.where(first_head, xz, x)], axis=0)

    def key_tile(kt):
        ks = pl.multiple_of(kt * ATT_TILE, ATT_TILE)
        return per_head(k_ref[pl.ds(ks, ATT_TILE), :]), per_head(v_ref[pl.ds(ks, ATT_TILE), :])

    def softplus(z):
        return jnp.maximum(z, 0.0) + jnp.log(1.0 + jnp.exp2(jnp.abs(z) * -LOG2E))

    def suffix_lhs(sp):
        hi, lo = _split_hi_lo(sp)
        return jnp.concatenate([hi, lo], axis=1)

    def row_totals(neg_sums):
        n = neg_sums.shape[0]
        return jnp.concatenate(
            [jnp.broadcast_to(neg_sums[:, 0:1], (n, ATT_TILE)),
             jnp.broadcast_to(neg_sums[:, ATT_TILE:ATT_TILE + 1], (n, ATT_TILE))], axis=1)

    def rows_used(i, j):
        return ATT_TILE if i - j < ATT_STATIC - 1 else ATT_TRIM

    def static_part(first_block):
        halo = 0 if first_block else ATT_STATIC - 1
        tiles = range(-halo, ATT_NSUB)
        users = {j: [i for i in range(ATT_NSUB) if 0 <= i - j < ATT_STATIC] for j in tiles}
        kv = {j: key_tile(qb * ATT_NSUB + j) for j in tiles}
        z = {}
        for j in tiles:
            r0 = users[j][0] * ATT_TILE
            n = sum(rows_used(i, j) for i in users[j])
            zz = lax.dot_general(q_ref[r0:r0 + n, :], kv[j][0], contract_last,
                                 preferred_element_type=F32)
            off = 0
            for i in users[j]:
                z[i, j] = zz[off:off + rows_used(i, j)]
                off += rows_used(i, j)
        acc = [None] * ATT_NSUB
        finals = []
        groups = [range(g, g + ATT_GROUP) for g in range(0, ATT_NSUB, ATT_GROUP)]

        def suffix_sums(group):
            pairs = [(i, j) for i in group for j in range(i, i - ATT_STATIC, -1) if j >= -halo]
            lhs, start, off = [], {}, 0
            for i, j in pairs:
                if i == j:
                    z[i, j] = jnp.where(below_diag, z[i, j], -MASKED_SCORE)
                lhs.append(suffix_lhs(softplus(z[i, j])))
                start[i, j] = off
                off += rows_used(i, j)
            return start, jnp.dot(jnp.concatenate(lhs, axis=0), suf_ref[...],
                                  preferred_element_type=F32)

        def weigh_values(group, start, neg_sums):
            g = group[0]
            wb = {}
            for i in group:
                carry = None
                for j in range(i, i - ATT_STATIC, -1):
                    if j < -halo:
                        continue
                    n = rows_used(i, j)
                    ns = neg_sums[start[i, j]:start[i, j] + n]
                    arg = z[i, j] + ns
                    if carry is not None:
                        arg = arg + carry[:n]
                    wb[i, j] = jnp.exp(arg).astype(BF16)
                    tot = row_totals(ns)
                    if carry is None:
                        carry = tot
                    elif n == ATT_TILE:
                        carry = carry + tot
                    else:
                        carry = jnp.concatenate([carry[:n] + tot, carry[n:]], axis=0)
                carry_ref[i] = carry
                finals.append(carry)
            for j in range(g + ATT_GROUP - 1, max(g - ATT_STATIC, -halo - 1), -1):
                mine = [i for i in group if (i, j) in wb]
                if not mine:
                    continue
                out = jnp.dot(jnp.concatenate([wb[i, j] for i in mine], axis=0), kv[j][1],
                              preferred_element_type=F32)
                off = 0
                for i in mine:
                    n = rows_used(i, j)
                    part = out[off:off + n]
                    off += n
                    if acc[i] is None:
                        acc[i] = part
                    elif n == ATT_TILE:
                        acc[i] = acc[i] + part
                    else:
                        acc[i] = jnp.concatenate([acc[i][:n] + part, acc[i][n:]], axis=0)
        pending = []
        for group in groups:
            pending.append((group,) + suffix_sums(group))
            if len(pending) > ATT_AHEAD:
                weigh_values(*pending.pop(0))
        for waiting in pending:
            weigh_values(*waiting)
        live_ref[0] = (_tree_max(finals) > EXP_ZERO_BELOW).astype(jnp.int32)
        for i in range(ATT_NSUB):
            acc_ref[i] = acc[i]
            o_ref[i * ATT_TILE:(i + 1) * ATT_TILE, :] = acc[i].astype(BF16)

    @pl.when(qb == 0)
    def _():
        static_part(True)

    @pl.when(qb > 0)
    def _():
        static_part(False)

    @pl.when(live_ref[0] > 0)
    def _():
        def finish(i, r0, r1, first_tile):
            rows = pl.ds(pl.multiple_of(i * ATT_TILE + r0, ATT_TRIM), r1 - r0)
            q = q_ref[rows, :]

            def cond(state):
                kt, live, _, _ = state
                return jnp.logical_and(kt >= 0, live)

            def body(state):
                kt, _, carry, acc = state
                kk, vv = key_tile(kt)
                z = lax.dot_general(q, kk, contract_last, preferred_element_type=F32)
                ns = jnp.dot(suffix_lhs(softplus(z)), suf_ref[...], preferred_element_type=F32)
                w = jnp.exp(z + ns + carry).astype(BF16)
                acc = acc + jnp.dot(w, vv, preferred_element_type=F32)
                carry = carry + row_totals(ns)
                return kt - 1, jnp.max(carry) > EXP_ZERO_BELOW, carry, acc

            carry = carry_ref[i, r0:r1, :]
            state = (first_tile, jnp.max(carry) > EXP_ZERO_BELOW, carry, acc_ref[i, r0:r1, :])
            acc = lax.while_loop(cond, body, state)[3]
            o_ref[rows, :] = acc.astype(BF16)

        def finish_sub_tile(i, _):
            newest_unseen = qb * ATT_NSUB + i - ATT_STATIC
            finish(i, 0, ATT_TRIM, newest_unseen)
            finish(i, ATT_TRIM, ATT_TILE, newest_unseen + 1)
            return 0

        lax.fori_loop(0, ATT_NSUB, finish_sub_tile, 0)


def _attention(q, k, v, suffix, batch, seq):
    t = q.shape[0]
    n_q = seq // ATT_BLOCK
    q_spec = pl.BlockSpec((ATT_BLOCK, LANES), lambda b, h, i: (b * n_q + i, h))
    kv_spec = pl.BlockSpec((seq, LANES), lambda b, h, i: (b, h))
    return pl.pallas_call(
        _attn_kernel,
        grid=(batch, N_HEADS // HEADS_PER_BLOCK, n_q),
        in_specs=[q_spec, kv_spec, kv_spec,
                  _resident((4 * ATT_TILE, 2 * ATT_TILE), lambda b, h, i: (0, 0))],
        out_specs=q_spec,
        out_shape=jax.ShapeDtypeStruct((t, D_MODEL), BF16),
        scratch_shapes=[pltpu.VMEM((ATT_NSUB, ATT_TILE, 2 * ATT_TILE), F32),
                        pltpu.VMEM((ATT_NSUB, ATT_TILE, LANES), F32),
                        pltpu.SMEM((1,), jnp.int32)],
        compiler_params=_params("arbitrary", "arbitrary", "arbitrary"),
        name="attn",
    )(q, k, v, suffix)


def _attention_constants():
    j = lax.broadcasted_iota(jnp.int32, (2 * ATT_TILE, 2 * ATT_TILE), 0)
    s = lax.broadcasted_iota(jnp.int32, (2 * ATT_TILE, 2 * ATT_TILE), 1)
    same_head = (j // ATT_TILE) == (s // ATT_TILE)
    half = jnp.where(jnp.logical_and(same_head, j >= s), -1.0, 0.0).astype(BF16)
    suffix = jnp.concatenate([half, half], axis=0)
    r = lax.broadcasted_iota(jnp.int32, (MXU_WIDTH, MXU_WIDTH), 0) // HEAD_DIM
    c = lax.broadcasted_iota(jnp.int32, (MXU_WIDTH, MXU_WIDTH), 1) // HEAD_DIM
    avg = jnp.where(r == c, 1.0 / HEAD_DIM, 0.0).astype(BF16)
    return suffix, avg


def kernel(x, p, norm_ffn1, w_ffn1_gu, w_ffn1_down, norm_mix, w_qkv, q_norm, k_norm, w_o,
           w_pool_in, w_pool_grp, pool_scale, norm_ffn2, w_ffn2_gu, w_ffn2_down, norm_ple,
           w_ple_gate, w_ple_proj):
    batch, seq, _ = x.shape
    depth = norm_ffn1.shape[0]
    t = batch * seq
    bf = lambda w: w.astype(BF16)
    row3 = lambda g: g.reshape(g.shape[0], 1, g.shape[1])
    assert t == N_ROW_STEPS * ROW_TILE
    wqkv, wo, wpin, wpgrp = bf(w_qkv), bf(w_o), bf(w_pool_in), bf(w_pool_grp)
    wpg, wpp = bf(w_ple_gate), bf(w_ple_proj)
    n1, nm, n2, npl = row3(norm_ffn1), row3(norm_mix), row3(norm_ffn2), row3(norm_ple)
    qg = row3(jnp.tile(q_norm, (1, N_HEADS)))
    kg = row3(jnp.tile(k_norm, (1, N_HEADS)))
    psc = row3(pool_scale)
    suffix, avg = _attention_constants()
    p2 = p.reshape(depth, t, PLE_DIM)

    xs = x.reshape(t, D_MODEL)
    ffn_w = (bf(w_ffn1_gu[0]), bf(w_ffn1_down[0]))
    for i in range(depth):
        j = i // 2
        cast2 = (w_ffn2_gu, w_ffn2_down, i)
        if i % 2 == 0:
            x1, q, k, v, *ffn_w = _ffn_qkv(xs, n1, ffn_w, nm, wqkv, qg, kg, avg, i, j, cast2)
            o = _attention(q, k, v, suffix, batch, seq)
        else:
            x1, *ffn_w = _ffn_pool(xs, n1, ffn_w, nm, wpin, wpgrp, psc, batch, seq, i, j, cast2)
            o = None
        cast1 = (w_ffn1_gu, w_ffn1_down, i + 1) if i + 1 < depth else None
        xs, *ffn_w = _ffn_ple(x1, o, wo, n2, ffn_w, p2, npl, wpg, wpp, i, j, cast1)
    return xs.reshape(batch, seq, D_MODEL)
```

```python
import jax
import jax.numpy as jnp
from jax import lax
from jax.experimental import pallas as pl
from jax.experimental.pallas import tpu as pltpu

D_MODEL = 1024
N_HEADS = 16
HEAD_DIM = D_MODEL // N_HEADS
D_FF = 2816
PLE_DIM = 256
POOL_WINDOWS = (2, 4, 8, 16)
POOL_GROUP = D_MODEL // len(POOL_WINDOWS)
EPS = 1e-6

LANES = 128
HEADS_PER_BLOCK = LANES // HEAD_DIM
MXU_WIDTH = 256
FF_CHUNK = MXU_WIDTH
N_FF_CHUNKS = D_FF // FF_CHUNK
ROW_TILE = 512
N_ROW_STEPS = 32
CAST_GU_ROWS = D_MODEL // N_ROW_STEPS
CAST_DOWN_ROWS = 2 * D_FF // N_ROW_STEPS
ATT_TILE = 128
ATT_NSUB = 8
ATT_BLOCK = ATT_TILE * ATT_NSUB
ATT_STATIC = 3
ATT_TRIM = 32
ATT_GROUP = 2
ATT_AHEAD = 2
MASKED_SCORE = 1e30
LOG2E = 1.4426950408889634
POOL_HALO = 32
assert all(w == 2 << i for i, w in enumerate(POOL_WINDOWS)) and POOL_HALO == 8 * len(POOL_WINDOWS)
POOL_LEVEL_COLS = sum(range(len(POOL_WINDOWS))) * POOL_GROUP
EXP_ZERO_BELOW = -104.0
VMEM_LIMIT = 56 * 1024 * 1024

F32 = jnp.float32
BF16 = jnp.bfloat16


def _rms(x, g):
    ms = jnp.mean(x * x, axis=-1, keepdims=True)
    return x * lax.rsqrt(ms + EPS) * g


def _params(*sem):
    return pltpu.CompilerParams(dimension_semantics=sem, vmem_limit_bytes=VMEM_LIMIT)


def _resident(shape, index_map):
    return pl.BlockSpec(shape, index_map, pipeline_mode=pl.Buffered(1))


def _layer_block(shape, layer):
    zeros = (0,) * len(shape)
    return _resident((None,) + tuple(shape), lambda *_: (layer,) + zeros)


def _split_hi_lo(x):
    hi = lax.bitcast_convert_type(
        lax.bitcast_convert_type(x, jnp.uint32) & jnp.uint32(0xFFFF0000), F32)
    return hi.astype(BF16), (x - hi).astype(BF16)


def _ffn_stage(x, g_ref, wgu_ref, wd_ref, a_ref):
    h = _rms(x, g_ref[...]).astype(BF16)
    for c in range(N_FF_CHUNKS):
        lo = c * FF_CHUNK
        gate = jnp.dot(h, wgu_ref[:, lo:lo + FF_CHUNK], preferred_element_type=F32)
        up = jnp.dot(h, wgu_ref[:, D_FF + lo:D_FF + lo + FF_CHUNK], preferred_element_type=F32)
        a_ref[:, lo:lo + FF_CHUNK] = (gate * jax.nn.sigmoid(gate) * up).astype(BF16)
    return x + 0.5 * jnp.dot(a_ref[...], wd_ref[...], preferred_element_type=F32)


def _ple_stage(x, p_ref, g_ref, wg_ref, wp_ref):
    h = _rms(x, g_ref[...]).astype(BF16)
    gate = jax.nn.sigmoid(jnp.dot(h, wg_ref[...], preferred_element_type=F32))
    proj = jnp.dot(p_ref[...].astype(BF16), wp_ref[...], preferred_element_type=F32)
    return x + gate * proj


def _head_norm(y, gain, avg):
    width = avg.shape[0]
    cols = []
    for c in range(D_MODEL // width):
        blk = y[:, c * width:(c + 1) * width]
        ms = jnp.dot((blk * blk).astype(BF16), avg, preferred_element_type=F32)
        cols.append(blk * lax.rsqrt(ms + EPS))
    return jnp.concatenate(cols, axis=1) * gain


def _qkv_stage(x, g_ref, w_ref, qg_ref, kg_ref, avg_ref, q_ref, k_ref, v_ref):
    h = _rms(x, g_ref[...]).astype(BF16)
    avg = avg_ref[...]
    q = jnp.dot(h, w_ref[:, 0:D_MODEL], preferred_element_type=F32)
    q_ref[...] = (_head_norm(q, qg_ref[...], avg) * (HEAD_DIM ** -0.5)).astype(BF16)
    k = jnp.dot(h, w_ref[:, D_MODEL:2 * D_MODEL], preferred_element_type=F32)
    k_ref[...] = _head_norm(k, kg_ref[...], avg).astype(BF16)
    v = jnp.dot(h, w_ref[:, 2 * D_MODEL:3 * D_MODEL], preferred_element_type=F32)
    v_ref[...] = v.astype(BF16)


def _pool_stage(x, si, g_ref, win_ref, wgrp_ref, sc_ref, o_ref, hist_ref, lvl_ref):
    h = _rms(x, g_ref[...]).astype(BF16)
    u = jnp.dot(h, win_ref[...], preferred_element_type=F32)

    @pl.when(si == 0)
    def _():
        hist_ref[0:POOL_HALO, :] = jnp.zeros((POOL_HALO, D_MODEL), F32)

    @pl.when(si > 0)
    def _():
        hist_ref[0:POOL_HALO, :] = hist_ref[ROW_TILE:ROW_TILE + POOL_HALO, :]

    hist_ref[POOL_HALO:POOL_HALO + ROW_TILE, :] = u
    n_rows = POOL_HALO + ROW_TILE
    n_groups = len(POOL_WINDOWS)
    pos = si * ROW_TILE + lax.broadcasted_iota(jnp.int32, (ROW_TILE, 1), 0)

    def lvl_col(k, gi):
        return sum(n_groups - m for m in range(1, k)) * POOL_GROUP + (gi - k) * POOL_GROUP

    def shifted_sum(k, gi, r0):
        shift = 1 << (k - 1)
        if k == 1:
            src, c = hist_ref, gi * POOL_GROUP
        else:
            src, c = lvl_ref, lvl_col(k - 1, gi)
        return (src[r0:n_rows, c:c + POOL_GROUP]
                + src[r0 - shift:n_rows - shift, c:c + POOL_GROUP])

    for k in range(1, n_groups + 1):
        for gi in range(k, n_groups):
            c = lvl_col(k, gi)
            lvl_ref[8 * k:n_rows, c:c + POOL_GROUP] = shifted_sum(k, gi, 8 * k)
        gi, win = k - 1, POOL_WINDOWS[k - 1]
        lo = gi * POOL_GROUP
        wsum = shifted_sum(k, gi, POOL_HALO)
        cnt = jnp.minimum(pos + 1, win).astype(F32)
        pooled = wsum / cnt - u[:, lo:lo + POOL_GROUP]
        y = jnp.dot(pooled.astype(BF16), wgrp_ref[gi], preferred_element_type=F32)
        o_ref[:, lo:lo + POOL_GROUP] = x[:, lo:lo + POOL_GROUP] + y * sc_ref[:, lo:lo + POOL_GROUP]


def _with_weight_cast(body, n_in, n_out, n_cast):
    def kernel(*refs):
        ins, src = refs[:n_in], refs[n_in:n_in + n_cast]
        outs = refs[n_in + n_cast:n_in + n_cast + n_out]
        dst = refs[n_in + n_cast + n_out:n_in + 2 * n_cast + n_out]
        scratch = refs[n_in + 2 * n_cast + n_out:]
        body(*ins, *outs, *scratch)
        for s, d in zip(src, dst):
            d[...] = s[...].astype(BF16)
    return kernel


def _ffn_casts(w_gu, w_down, layer):
    return [(w_gu, layer, CAST_GU_ROWS, 1), (w_down, layer, CAST_DOWN_ROWS, 2)]


def _per_token_call(body, name, grid, step, in_specs, args, out_specs, out_shape, scratch, casts):
    in_specs, out_specs, out_shape = list(in_specs), list(out_specs), list(out_shape)
    if casts:
        body = _with_weight_cast(body, len(in_specs), len(out_specs), len(casts))
    for src, layer, rows, per in casts:
        total, width = src.shape[-2:]
        assert total * per == rows * N_ROW_STEPS
        if layer is None:
            in_specs.append(pl.BlockSpec(
                (rows, width), lambda *g, per=per: (step(*g) // per, 0)))
        else:
            in_specs.append(pl.BlockSpec(
                (None, rows, width), lambda *g, per=per, layer=layer: (layer, step(*g) // per, 0)))
    for src, layer, rows, per in casts:
        total, width = src.shape[-2:]
        out_specs.append(pl.BlockSpec((rows, width), lambda *g, per=per: (step(*g) // per, 0)))
        out_shape.append(jax.ShapeDtypeStruct((total, width), BF16))
    args = tuple(args) + tuple(c[0] for c in casts)
    return pl.pallas_call(
        body,
        grid=grid,
        in_specs=in_specs,
        out_specs=out_specs,
        out_shape=out_shape,
        scratch_shapes=scratch,
        compiler_params=_params(*(["arbitrary"] * len(grid))),
        name=name,
    )(*args)


def _ffn_weight_specs():
    return [_resident((D_MODEL, 2 * D_FF), lambda *_: (0, 0)),
            _resident((D_FF, D_MODEL), lambda *_: (0, 0))]


def _ffn_qkv_kernel(x_ref, g1_ref, wgu_ref, wd_ref, gm_ref, wqkv_ref, qg_ref, kg_ref, avg_ref,
                    x1_ref, q_ref, k_ref, v_ref, a_ref):
    x1 = _ffn_stage(x_ref[...], g1_ref, wgu_ref, wd_ref, a_ref)
    x1_ref[...] = x1
    _qkv_stage(x1, gm_ref, wqkv_ref, qg_ref, kg_ref, avg_ref, q_ref, k_ref, v_ref)


def _ffn_qkv(x, n1, ffn_w, nm, w_qkv, q_gain, k_gain, avg, layer, j, cast):
    t = x.shape[0]
    row = pl.BlockSpec((ROW_TILE, D_MODEL), lambda m: (m, 0))
    act = jax.ShapeDtypeStruct((t, D_MODEL), BF16)
    return _per_token_call(
        _ffn_qkv_kernel, "ffn_qkv", (t // ROW_TILE,), lambda m: m,
        [row, _layer_block((1, D_MODEL), layer)] + _ffn_weight_specs() + [
            _layer_block((1, D_MODEL), layer),
            _resident((D_MODEL, 3 * D_MODEL), lambda *_: (0, 0)),
            _layer_block((1, D_MODEL), j),
            _layer_block((1, D_MODEL), j),
            _resident((MXU_WIDTH, MXU_WIDTH), lambda m: (0, 0)),
        ],
        (x, n1) + tuple(ffn_w) + (nm, w_qkv, q_gain, k_gain, avg),
        [row, row, row, row],
        [jax.ShapeDtypeStruct((t, D_MODEL), F32), act, act, act],
        [pltpu.VMEM((ROW_TILE, D_FF), BF16)],
        cast)


def _ffn_pool_kernel(x_ref, g1_ref, wgu_ref, wd_ref, gm_ref, win_ref, wgrp_ref, sc_ref,
                     o_ref, a_ref, hist_ref, lvl_ref):
    x1 = _ffn_stage(x_ref[...], g1_ref, wgu_ref, wd_ref, a_ref)
    _pool_stage(x1, pl.program_id(1), gm_ref, win_ref, wgrp_ref, sc_ref, o_ref, hist_ref, lvl_ref)


def _ffn_pool(x, n1, ffn_w, nm, w_in, w_grp, scale, batch, seq, layer, j, cast):
    t = x.shape[0]
    n_s = seq // ROW_TILE
    row = pl.BlockSpec((ROW_TILE, D_MODEL), lambda b, s: (b * n_s + s, 0))
    n_g = len(POOL_WINDOWS)
    return _per_token_call(
        _ffn_pool_kernel, "ffn_pool", (batch, n_s), lambda b, s: b * n_s + s,
        [row, _layer_block((1, D_MODEL), layer)] + _ffn_weight_specs() + [
            _layer_block((1, D_MODEL), layer),
            _layer_block((D_MODEL, D_MODEL), j),
            _layer_block((n_g, POOL_GROUP, POOL_GROUP), j),
            _layer_block((1, D_MODEL), j),
        ],
        (x, n1) + tuple(ffn_w) + (nm, w_in, w_grp, scale),
        [row],
        [jax.ShapeDtypeStruct((t, D_MODEL), F32)],
        [pltpu.VMEM((ROW_TILE, D_FF), BF16),
         pltpu.VMEM((ROW_TILE + POOL_HALO, D_MODEL), F32),
         pltpu.VMEM((ROW_TILE + POOL_HALO, POOL_LEVEL_COLS), F32)],
        cast)


def _proj_ffn_ple_kernel(x_ref, o_ref, wo_ref, g2_ref, wgu_ref, wd_ref, p_ref, gp_ref, wg_ref,
                         wp_ref, y_ref, a_ref):
    x = x_ref[...] + jnp.dot(o_ref[...], wo_ref[...], preferred_element_type=F32)
    x = _ffn_stage(x, g2_ref, wgu_ref, wd_ref, a_ref)
    y_ref[...] = _ple_stage(x, p_ref, gp_ref, wg_ref, wp_ref)


def _ffn_ple_kernel(x_ref, g2_ref, wgu_ref, wd_ref, p_ref, gp_ref, wg_ref, wp_ref, y_ref, a_ref):
    x = _ffn_stage(x_ref[...], g2_ref, wgu_ref, wd_ref, a_ref)
    y_ref[...] = _ple_stage(x, p_ref, gp_ref, wg_ref, wp_ref)


def _ffn_ple(x, o, w_o, n2, ffn_w, p, npl, w_gate, w_proj, layer, j, cast):
    t = x.shape[0]
    row = pl.BlockSpec((ROW_TILE, D_MODEL), lambda m: (m, 0))
    tail_specs = [_layer_block((1, D_MODEL), layer)] + _ffn_weight_specs() + [
        pl.BlockSpec((None, ROW_TILE, PLE_DIM), lambda m: (layer, m, 0)),
        _layer_block((1, D_MODEL), layer),
        _layer_block((D_MODEL, D_MODEL), layer),
        _layer_block((PLE_DIM, D_MODEL), layer),
    ]
    tail_args = (n2,) + tuple(ffn_w) + (p, npl, w_gate, w_proj)
    if o is None:
        body, specs, args, name = _ffn_ple_kernel, [row] + tail_specs, (x,) + tail_args, "ffn_ple"
    else:
        body, name = _proj_ffn_ple_kernel, "proj_ffn_ple"
        specs = [row, row, _layer_block((D_MODEL, D_MODEL), j)] + tail_specs
        args = (x, o, w_o) + tail_args
    return _per_token_call(
        body, name, (t // ROW_TILE,), lambda m: m, specs, args,
        [row], [jax.ShapeDtypeStruct((t, D_MODEL), F32)],
        [pltpu.VMEM((ROW_TILE, D_FF), BF16)], cast)


def _tree_max(blocks):
    rows = [b[r:r + 8, c:c + LANES] for b in blocks
            for r in range(0, b.shape[0], 8) for c in range(0, b.shape[1], LANES)]
    while len(rows) > 1:
        rows = [jnp.maximum(a, b) for a, b in zip(rows[0::2], rows[1::2])] + rows[len(rows) & ~1:]
    return jnp.max(rows[0])


def _attn_kernel(q_ref, k_ref, v_ref, suf_ref, o_ref, carry_ref, acc_ref, live_ref):
    qb = pl.program_id(2)
    lane = lax.broadcasted_iota(jnp.int32, (ATT_TILE, LANES), 1)
    first_head = lane < HEAD_DIM
    row = lax.broadcasted_iota(jnp.int32, (ATT_TILE, 2 * ATT_TILE), 0)
    col = lax.broadcasted_iota(jnp.int32, (ATT_TILE, 2 * ATT_TILE), 1) & (ATT_TILE - 1)
    below_diag = col < row
    contract_last = (((1,), (1,)), ((), ()))

    def per_head(x):
        xz = jnp.zeros_like(x)
        return jnp.concatenate([jnp.where(first_head, x, xz), jnp.where(first_head, xz, x)], axis=0)

    def key_tile(kt):
        ks = pl.multiple_of(kt * ATT_TILE, ATT_TILE)
        return per_head(k_ref[pl.ds(ks, ATT_TILE), :]), per_head(v_ref[pl.ds(ks, ATT_TILE), :])

    def softplus(z):
        return jnp.maximum(z, 0.0) + jnp.log(1.0 + jnp.exp2(jnp.abs(z) * -LOG2E))

    def suffix_lhs(sp):
        return sp.astype(BF16)

    def row_totals(neg_sums):
        n = neg_sums.shape[0]
        return jnp.concatenate(
            [jnp.broadcast_to(neg_sums[:, 0:1], (n, ATT_TILE)),
             jnp.broadcast_to(neg_sums[:, ATT_TILE:ATT_TILE + 1], (n, ATT_TILE))], axis=1)

    def rows_used(i, j):
        return ATT_TILE if i - j < ATT_STATIC - 1 else ATT_TRIM

    def static_part(first_block):
        halo = 0 if first_block else ATT_STATIC - 1
        tiles = range(-halo, ATT_NSUB)
        users = {j: [i for i in range(ATT_NSUB) if 0 <= i - j < ATT_STATIC] for j in tiles}
        kv = {j: key_tile(qb * ATT_NSUB + j) for j in tiles}
        z = {}
        for j in tiles:
            r0 = users[j][0] * ATT_TILE
            n = sum(rows_used(i, j) for i in users[j])
            zz = lax.dot_general(q_ref[r0:r0 + n, :], kv[j][0], contract_last,
                                 preferred_element_type=F32)
            off = 0
            for i in users[j]:
                z[i, j] = zz[off:off + rows_used(i, j)]
                off += rows_used(i, j)
        acc = [None] * ATT_NSUB
        finals = []
        groups = [range(g, g + ATT_GROUP) for g in range(0, ATT_NSUB, ATT_GROUP)]

        def suffix_sums(group):
            pairs = [(i, j) for i in group for j in range(i, i - ATT_STATIC, -1) if j >= -halo]
            lhs, start, off = [], {}, 0
            for i, j in pairs:
                if i == j:
                    z[i, j] = jnp.where(below_diag, z[i, j], -MASKED_SCORE)
                lhs.append(suffix_lhs(softplus(z[i, j])))
                start[i, j] = off
                off += rows_used(i, j)
            return start, jnp.dot(jnp.concatenate(lhs, axis=0), suf_ref[...],
                                  preferred_element_type=F32)

        def weigh_values(group, start, neg_sums):
            g = group[0]
            wb = {}
            for i in group:
                carry = None
                for j in range(i, i - ATT_STATIC, -1):
                    if j < -halo:
                        continue
                    n = rows_used(i, j)
                    ns = neg_sums[start[i, j]:start[i, j] + n]
                    arg = z[i, j] + ns
                    if carry is not None:
                        arg = arg + carry[:n]
                    wb[i, j] = jnp.exp(arg).astype(BF16)
                    tot = row_totals(ns)
                    if carry is None:
                        carry = tot
                    elif n == ATT_TILE:
                        carry = carry + tot
                    else:
                        carry = jnp.concatenate([carry[:n] + tot, carry[n:]], axis=0)
                carry_ref[i] = carry
                finals.append(carry)
            for j in range(g + ATT_GROUP - 1, max(g - ATT_STATIC, -halo - 1), -1):
                mine = [i for i in group if (i, j) in wb]
                if not mine:
                    continue
                out = jnp.dot(jnp.concatenate([wb[i, j] for i in mine], axis=0), kv[j][1],
                              preferred_element_type=F32)
                off = 0
                for i in mine:
                    n = rows_used(i, j)
                    part = out[off:off + n]
                    off += n
                    if acc[i] is None:
                        acc[i] = part
                    elif n == ATT_TILE:
                        acc[i] = acc[i] + part
                    else:
                        acc[i] = jnp.concatenate([acc[i][:n] + part, acc[i][n:]], axis=0)
        pending = []
        for group in groups:
            pending.append((group,) + suffix_sums(group))
            if len(pending) > ATT_AHEAD:
                weigh_values(*pending.pop(0))
        for waiting in pending:
            weigh_values(*waiting)
        live_ref[0] = (_tree_max(finals) > EXP_ZERO_BELOW).astype(jnp.int32)
        for i in range(ATT_NSUB):
            acc_ref[i] = acc[i]
            o_ref[i * ATT_TILE:(i + 1) * ATT_TILE, :] = acc[i].astype(BF16)

    @pl.when(qb == 0)
    def _():
        static_part(True)

    @pl.when(qb > 0)
    def _():
        static_part(False)

    @pl.when(live_ref[0] > 0)
    def _():
        def finish(i, r0, r1, first_tile):
            rows = pl.ds(pl.multiple_of(i * ATT_TILE + r0, ATT_TRIM), r1 - r0)
            q = q_ref[rows, :]

            def cond(state):
                kt, live, _, _ = state
                return jnp.logical_and(kt >= 0, live)

            def body(state):
                kt, _, carry, acc = state
                kk, vv = key_tile(kt)
                z = lax.dot_general(q, kk, contract_last, preferred_element_type=F32)
                ns = jnp.dot(suffix_lhs(softplus(z)), suf_ref[...], preferred_element_type=F32)
                w = jnp.exp(z + ns + carry).astype(BF16)
                acc = acc + jnp.dot(w, vv, preferred_element_type=F32)
                carry = carry + row_totals(ns)
                return kt - 1, jnp.max(carry) > EXP_ZERO_BELOW, carry, acc

            carry = carry_ref[i, r0:r1, :]
            state = (first_tile, jnp.max(carry) > EXP_ZERO_BELOW, carry, acc_ref[i, r0:r1, :])
            acc = lax.while_loop(cond, body, state)[3]
            o_ref[rows, :] = acc.astype(BF16)

        def finish_sub_tile(i, _):
            newest_unseen = qb * ATT_NSUB + i - ATT_STATIC
            finish(i, 0, ATT_TRIM, newest_unseen)
            finish(i, ATT_TRIM, ATT_TILE, newest_unseen + 1)
            return 0

        lax.fori_loop(0, ATT_NSUB, finish_sub_tile, 0)


def _attention(q, k, v, suffix, batch, seq):
    t = q.shape[0]
    n_q = seq // ATT_BLOCK
    q_spec = pl.BlockSpec((ATT_BLOCK, LANES), lambda b, h, i: (b * n_q + i, h))
    kv_spec = pl.BlockSpec((seq, LANES), lambda b, h, i: (b, h))
    return pl.pallas_call(
        _attn_kernel,
        grid=(batch, N_HEADS // HEADS_PER_BLOCK, n_q),
        in_specs=[q_spec, kv_spec, kv_spec,
                  _resident((2 * ATT_TILE, 2 * ATT_TILE), lambda b, h, i: (0, 0))],
        out_specs=q_spec,
        out_shape=jax.ShapeDtypeStruct((t, D_MODEL), BF16),
        scratch_shapes=[pltpu.VMEM((ATT_NSUB, ATT_TILE, 2 * ATT_TILE), F32),
                        pltpu.VMEM((ATT_NSUB, ATT_TILE, LANES), F32),
                        pltpu.SMEM((1,), jnp.int32)],
        compiler_params=_params("arbitrary", "arbitrary", "arbitrary"),
        name="attn",
    )(q, k, v, suffix)


def _attention_constants():
    j = lax.broadcasted_iota(jnp.int32, (2 * ATT_TILE, 2 * ATT_TILE), 0)
    s = lax.broadcasted_iota(jnp.int32, (2 * ATT_TILE, 2 * ATT_TILE), 1)
    same_head = (j // ATT_TILE) == (s // ATT_TILE)
    suffix = jnp.where(jnp.logical_and(same_head, j >= s), -1.0, 0.0).astype(BF16)
    r = lax.broadcasted_iota(jnp.int32, (MXU_WIDTH, MXU_WIDTH), 0) // HEAD_DIM
    c = lax.broadcasted_iota(jnp.int32, (MXU_WIDTH, MXU_WIDTH), 1) // HEAD_DIM
    avg = jnp.where(r == c, 1.0 / HEAD_DIM, 0.0).astype(BF16)
    return suffix, avg


def kernel(x, p, norm_ffn1, w_ffn1_gu, w_ffn1_down, norm_mix, w_qkv, q_norm, k_norm, w_o,
           w_pool_in, w_pool_grp, pool_scale, norm_ffn2, w_ffn2_gu, w_ffn2_down, norm_ple,
           w_ple_gate, w_ple_proj):
    batch, seq, _ = x.shape
    depth = norm_ffn1.shape[0]
    t = batch * seq
    bf = lambda w: w.astype(BF16)
    row3 = lambda g: g.reshape(g.shape[0], 1, g.shape[1])
    assert t == N_ROW_STEPS * ROW_TILE
    n1, nm, n2, npl = row3(norm_ffn1), row3(norm_mix), row3(norm_ffn2), row3(norm_ple)
    qg = row3(jnp.tile(q_norm, (1, N_HEADS)))
    kg = row3(jnp.tile(k_norm, (1, N_HEADS)))
    psc = row3(pool_scale)
    suffix, avg = _attention_constants()
    p2 = p.reshape(depth, t, PLE_DIM)

    xs = x.reshape(t, D_MODEL)
    ffn_w = (bf(w_ffn1_gu[0]), bf(w_ffn1_down[0]))
    wqkv = bf(w_qkv[0])
    merged = [w.reshape(-1, w.shape[-1])
              for w in (w_o, w_pool_in, w_pool_grp, w_ple_gate, w_ple_proj)]
    small = [(w_qkv, 1, D_MODEL // N_ROW_STEPS, 1)] + [
        (w, None, w.shape[0] // N_ROW_STEPS, 1) for w in merged]
    for i in range(depth):
        j = i // 2
        casts = _ffn_casts(w_ffn2_gu, w_ffn2_down, i)
        if i == 0:
            x1, q, k, v, *ffn_w, wqkv_next, wo, wpin, wpgrp, wpg, wpp = _ffn_qkv(
                xs, n1, ffn_w, nm, wqkv, qg, kg, avg, i, j, casts + small)
            wo, wpin, wpgrp, wpg, wpp = (
                c.reshape(w.shape) for c, w in
                zip((wo, wpin, wpgrp, wpg, wpp), (w_o, w_pool_in, w_pool_grp, w_ple_gate, w_ple_proj)))
        elif i % 2 == 0:
            x1, q, k, v, *ffn_w = _ffn_qkv(xs, n1, ffn_w, nm, wqkv_next, qg, kg, avg, i, j, casts)
        else:
            x1, *ffn_w = _ffn_pool(xs, n1, ffn_w, nm, wpin, wpgrp, psc, batch, seq, i, j, casts)
        o = _attention(q, k, v, suffix, batch, seq) if i % 2 == 0 else None
        casts = _ffn_casts(w_ffn1_gu, w_ffn1_down, i + 1) if i + 1 < depth else []
        xs, *ffn_w = _ffn_ple(x1, o, wo, n2, ffn_w, p2, npl, wpg, wpp, i, j, casts)
    return xs.reshape(batch, seq, D_MODEL)
```

```python
import jax
import jax.numpy as jnp
from jax import lax
from jax.experimental import pallas as pl
from jax.experimental.pallas import tpu as pltpu

D_MODEL = 1024
N_HEADS = 16
HEAD_DIM = D_MODEL // N_HEADS
D_FF = 2816
PLE_DIM = 256
POOL_WINDOWS = (2, 4, 8, 16)
POOL_GROUP = D_MODEL // len(POOL_WINDOWS)
EPS = 1e-6

LANES = 128
HEADS_PER_BLOCK = LANES // HEAD_DIM
N_HEAD_PAIRS = N_HEADS // HEADS_PER_BLOCK
MXU_WIDTH = 256
FF_CHUNK = MXU_WIDTH
N_FF_CHUNKS = D_FF // FF_CHUNK
ROW_TILE = 512
N_ROW_STEPS = 32
CAST_GU_ROWS = D_MODEL // N_ROW_STEPS
CAST_DOWN_ROWS = 2 * D_FF // N_ROW_STEPS
ATT_TILE = 128
ATT_NSUB = 8
ATT_BLOCK = ATT_TILE * ATT_NSUB
ATT_STATIC = 3
ATT_TRIM = 32
ATT_GROUP = 2
ATT_AHEAD = 2
MASKED_SCORE = 1e30
LOG2E = 1.4426950408889634
POOL_HALO = 32
assert all(w == 2 << i for i, w in enumerate(POOL_WINDOWS)) and POOL_HALO == 8 * len(POOL_WINDOWS)
POOL_LEVEL_COLS = sum(range(len(POOL_WINDOWS))) * POOL_GROUP
EXP_ZERO_BELOW = -104.0
VMEM_LIMIT = 56 * 1024 * 1024

F32 = jnp.float32
BF16 = jnp.bfloat16


def _rms(x, g):
    ms = jnp.mean(x * x, axis=-1, keepdims=True)
    return x * lax.rsqrt(ms + EPS) * g


def _params(*sem):
    return pltpu.CompilerParams(dimension_semantics=sem, vmem_limit_bytes=VMEM_LIMIT)


def _resident(shape, index_map):
    return pl.BlockSpec(shape, index_map, pipeline_mode=pl.Buffered(1))


def _layer_block(shape, layer):
    zeros = (0,) * len(shape)
    return _resident((None,) + tuple(shape), lambda *_: (layer,) + zeros)


def _split_hi_lo(x):
    hi = lax.bitcast_convert_type(
        lax.bitcast_convert_type(x, jnp.uint32) & jnp.uint32(0xFFFF0000), F32)
    return hi.astype(BF16), (x - hi).astype(BF16)


def _ffn_stage(x, g_ref, wgu_ref, wd_ref, a_ref):
    h = _rms(x, g_ref[...]).astype(BF16)
    for c in range(N_FF_CHUNKS):
        lo = c * FF_CHUNK
        gate = jnp.dot(h, wgu_ref[:, lo:lo + FF_CHUNK], preferred_element_type=F32)
        up = jnp.dot(h, wgu_ref[:, D_FF + lo:D_FF + lo + FF_CHUNK], preferred_element_type=F32)
        a_ref[:, lo:lo + FF_CHUNK] = (gate * jax.nn.sigmoid(gate) * up).astype(BF16)
    return x + 0.5 * jnp.dot(a_ref[...], wd_ref[...], preferred_element_type=F32)


def _ple_stage(x, p_ref, g_ref, wg_ref, wp_ref):
    h = _rms(x, g_ref[...]).astype(BF16)
    gate = jax.nn.sigmoid(jnp.dot(h, wg_ref[...], preferred_element_type=F32))
    proj = jnp.dot(p_ref[...].astype(BF16), wp_ref[...], preferred_element_type=F32)
    return x + gate * proj


def _head_norm(y, gain, avg):
    width = avg.shape[0]
    cols = []
    for c in range(D_MODEL // width):
        blk = y[:, c * width:(c + 1) * width]
        ms = jnp.dot((blk * blk).astype(BF16), avg, preferred_element_type=F32)
        cols.append(blk * lax.rsqrt(ms + EPS))
    return jnp.concatenate(cols, axis=1) * gain


def _qkv_stage(x, g_ref, w_ref, qg_ref, kg_ref, avg_ref, q_ref, k_ref, v_ref):
    h = _rms(x, g_ref[...]).astype(BF16)
    avg = avg_ref[...]
    def put(ref, y):
        for hp in range(N_HEAD_PAIRS):
            ref[hp] = y[:, hp * LANES:(hp + 1) * LANES]

    q = jnp.dot(h, w_ref[:, 0:D_MODEL], preferred_element_type=F32)
    put(q_ref, (_head_norm(q, qg_ref[...], avg) * (HEAD_DIM ** -0.5)).astype(BF16))
    k = jnp.dot(h, w_ref[:, D_MODEL:2 * D_MODEL], preferred_element_type=F32)
    put(k_ref, _head_norm(k, kg_ref[...], avg).astype(BF16))
    v = jnp.dot(h, w_ref[:, 2 * D_MODEL:3 * D_MODEL], preferred_element_type=F32)
    put(v_ref, v.astype(BF16))


def _pool_stage(x, si, g_ref, win_ref, wgrp_ref, sc_ref, o_ref, hist_ref, lvl_ref):
    h = _rms(x, g_ref[...]).astype(BF16)
    u = jnp.dot(h, win_ref[...], preferred_element_type=F32)

    @pl.when(si == 0)
    def _():
        hist_ref[0:POOL_HALO, :] = jnp.zeros((POOL_HALO, D_MODEL), F32)

    @pl.when(si > 0)
    def _():
        hist_ref[0:POOL_HALO, :] = hist_ref[ROW_TILE:ROW_TILE + POOL_HALO, :]

    hist_ref[POOL_HALO:POOL_HALO + ROW_TILE, :] = u
    n_rows = POOL_HALO + ROW_TILE
    n_groups = len(POOL_WINDOWS)
    pos = si * ROW_TILE + lax.broadcasted_iota(jnp.int32, (ROW_TILE, 1), 0)

    def lvl_col(k, gi):
        return sum(n_groups - m for m in range(1, k)) * POOL_GROUP + (gi - k) * POOL_GROUP

    def shifted_sum(k, gi, r0):
        shift = 1 << (k - 1)
        if k == 1:
            src, c = hist_ref, gi * POOL_GROUP
        else:
            src, c = lvl_ref, lvl_col(k - 1, gi)
        return (src[r0:n_rows, c:c + POOL_GROUP]
                + src[r0 - shift:n_rows - shift, c:c + POOL_GROUP])

    for k in range(1, n_groups + 1):
        for gi in range(k, n_groups):
            c = lvl_col(k, gi)
            lvl_ref[8 * k:n_rows, c:c + POOL_GROUP] = shifted_sum(k, gi, 8 * k)
        gi, win = k - 1, POOL_WINDOWS[k - 1]
        lo = gi * POOL_GROUP
        wsum = shifted_sum(k, gi, POOL_HALO)
        cnt = jnp.minimum(pos + 1, win).astype(F32)
        pooled = wsum / cnt - u[:, lo:lo + POOL_GROUP]
        y = jnp.dot(pooled.astype(BF16), wgrp_ref[gi], preferred_element_type=F32)
        o_ref[:, lo:lo + POOL_GROUP] = x[:, lo:lo + POOL_GROUP] + y * sc_ref[:, lo:lo + POOL_GROUP]


def _with_weight_cast(body, n_in, n_out, n_cast):
    def kernel(*refs):
        ins, src = refs[:n_in], refs[n_in:n_in + n_cast]
        outs = refs[n_in + n_cast:n_in + n_cast + n_out]
        dst = refs[n_in + n_cast + n_out:n_in + 2 * n_cast + n_out]
        scratch = refs[n_in + 2 * n_cast + n_out:]
        body(*ins, *outs, *scratch)
        for s, d in zip(src, dst):
            d[...] = s[...].astype(BF16)
    return kernel


def _ffn_casts(w_gu, w_down, layer):
    return [(w_gu, layer, CAST_GU_ROWS, 1), (w_down, layer, CAST_DOWN_ROWS, 2)]


def _per_token_call(body, name, grid, step, in_specs, args, out_specs, out_shape, scratch, casts):
    in_specs, out_specs, out_shape = list(in_specs), list(out_specs), list(out_shape)
    if casts:
        body = _with_weight_cast(body, len(in_specs), len(out_specs), len(casts))
    for src, layer, rows, per in casts:
        total, width = src.shape[-2:]
        assert total * per == rows * N_ROW_STEPS
        if layer is None:
            in_specs.append(pl.BlockSpec(
                (rows, width), lambda *g, per=per: (step(*g) // per, 0)))
        else:
            in_specs.append(pl.BlockSpec(
                (None, rows, width), lambda *g, per=per, layer=layer: (layer, step(*g) // per, 0)))
    for src, layer, rows, per in casts:
        total, width = src.shape[-2:]
        out_specs.append(pl.BlockSpec((rows, width), lambda *g, per=per: (step(*g) // per, 0)))
        out_shape.append(jax.ShapeDtypeStruct((total, width), BF16))
    args = tuple(args) + tuple(c[0] for c in casts)
    return pl.pallas_call(
        body,
        grid=grid,
        in_specs=in_specs,
        out_specs=out_specs,
        out_shape=out_shape,
        scratch_shapes=scratch,
        compiler_params=_params(*(["arbitrary"] * len(grid))),
        name=name,
    )(*args)


def _ffn_weight_specs():
    return [_resident((D_MODEL, 2 * D_FF), lambda *_: (0, 0)),
            _resident((D_FF, D_MODEL), lambda *_: (0, 0))]


def _ffn_qkv_kernel(x_ref, g1_ref, wgu_ref, wd_ref, gm_ref, wqkv_ref, qg_ref, kg_ref, avg_ref,
                    x1_ref, q_ref, k_ref, v_ref, a_ref):
    x1 = _ffn_stage(x_ref[...], g1_ref, wgu_ref, wd_ref, a_ref)
    x1_ref[...] = x1
    _qkv_stage(x1, gm_ref, wqkv_ref, qg_ref, kg_ref, avg_ref, q_ref, k_ref, v_ref)


def _ffn_qkv(x, n1, ffn_w, nm, w_qkv, q_gain, k_gain, avg, layer, j, cast):
    t = x.shape[0]
    row = pl.BlockSpec((ROW_TILE, D_MODEL), lambda m: (m, 0))
    act = jax.ShapeDtypeStruct((N_HEAD_PAIRS, t, LANES), BF16)
    act_spec = pl.BlockSpec((N_HEAD_PAIRS, ROW_TILE, LANES), lambda m: (0, m, 0))
    return _per_token_call(
        _ffn_qkv_kernel, "ffn_qkv", (t // ROW_TILE,), lambda m: m,
        [row, _layer_block((1, D_MODEL), layer)] + _ffn_weight_specs() + [
            _layer_block((1, D_MODEL), layer),
            _resident((D_MODEL, 3 * D_MODEL), lambda *_: (0, 0)),
            _layer_block((1, D_MODEL), j),
            _layer_block((1, D_MODEL), j),
            _resident((MXU_WIDTH, MXU_WIDTH), lambda m: (0, 0)),
        ],
        (x, n1) + tuple(ffn_w) + (nm, w_qkv, q_gain, k_gain, avg),
        [row, act_spec, act_spec, act_spec],
        [jax.ShapeDtypeStruct((t, D_MODEL), F32), act, act, act],
        [pltpu.VMEM((ROW_TILE, D_FF), BF16)],
        cast)


def _ffn_pool_kernel(x_ref, g1_ref, wgu_ref, wd_ref, gm_ref, win_ref, wgrp_ref, sc_ref,
                     o_ref, a_ref, hist_ref, lvl_ref):
    x1 = _ffn_stage(x_ref[...], g1_ref, wgu_ref, wd_ref, a_ref)
    _pool_stage(x1, pl.program_id(1), gm_ref, win_ref, wgrp_ref, sc_ref, o_ref, hist_ref, lvl_ref)


def _ffn_pool(x, n1, ffn_w, nm, w_in, w_grp, scale, batch, seq, layer, j, cast):
    t = x.shape[0]
    n_s = seq // ROW_TILE
    row = pl.BlockSpec((ROW_TILE, D_MODEL), lambda b, s: (b * n_s + s, 0))
    n_g = len(POOL_WINDOWS)
    return _per_token_call(
        _ffn_pool_kernel, "ffn_pool", (batch, n_s), lambda b, s: b * n_s + s,
        [row, _layer_block((1, D_MODEL), layer)] + _ffn_weight_specs() + [
            _layer_block((1, D_MODEL), layer),
            _layer_block((D_MODEL, D_MODEL), j),
            _layer_block((n_g, POOL_GROUP, POOL_GROUP), j),
            _layer_block((1, D_MODEL), j),
        ],
        (x, n1) + tuple(ffn_w) + (nm, w_in, w_grp, scale),
        [row],
        [jax.ShapeDtypeStruct((t, D_MODEL), F32)],
        [pltpu.VMEM((ROW_TILE, D_FF), BF16),
         pltpu.VMEM((ROW_TILE + POOL_HALO, D_MODEL), F32),
         pltpu.VMEM((ROW_TILE + POOL_HALO, POOL_LEVEL_COLS), F32)],
        cast)


def _proj_ffn_ple_kernel(x_ref, o_ref, wo_ref, g2_ref, wgu_ref, wd_ref, p_ref, gp_ref, wg_ref,
                         wp_ref, y_ref, a_ref):
    o = jnp.concatenate([o_ref[hp] for hp in range(N_HEAD_PAIRS)], axis=1)
    x = x_ref[...] + jnp.dot(o, wo_ref[...], preferred_element_type=F32)
    x = _ffn_stage(x, g2_ref, wgu_ref, wd_ref, a_ref)
    y_ref[...] = _ple_stage(x, p_ref, gp_ref, wg_ref, wp_ref)


def _ffn_ple_kernel(x_ref, g2_ref, wgu_ref, wd_ref, p_ref, gp_ref, wg_ref, wp_ref, y_ref, a_ref):
    x = _ffn_stage(x_ref[...], g2_ref, wgu_ref, wd_ref, a_ref)
    y_ref[...] = _ple_stage(x, p_ref, gp_ref, wg_ref, wp_ref)


def _ffn_ple(x, o, w_o, n2, ffn_w, p, npl, w_gate, w_proj, layer, j, cast):
    t = x.shape[0]
    row = pl.BlockSpec((ROW_TILE, D_MODEL), lambda m: (m, 0))
    tail_specs = [_layer_block((1, D_MODEL), layer)] + _ffn_weight_specs() + [
        pl.BlockSpec((None, ROW_TILE, PLE_DIM), lambda m: (layer, m, 0)),
        _layer_block((1, D_MODEL), layer),
        _layer_block((D_MODEL, D_MODEL), layer),
        _layer_block((PLE_DIM, D_MODEL), layer),
    ]
    tail_args = (n2,) + tuple(ffn_w) + (p, npl, w_gate, w_proj)
    if o is None:
        body, specs, args, name = _ffn_ple_kernel, [row] + tail_specs, (x,) + tail_args, "ffn_ple"
    else:
        body, name = _proj_ffn_ple_kernel, "proj_ffn_ple"
        o_spec = pl.BlockSpec((N_HEAD_PAIRS, ROW_TILE, LANES), lambda m: (0, m, 0))
        specs = [row, o_spec, _layer_block((D_MODEL, D_MODEL), j)] + tail_specs
        args = (x, o, w_o) + tail_args
    return _per_token_call(
        body, name, (t // ROW_TILE,), lambda m: m, specs, args,
        [row], [jax.ShapeDtypeStruct((t, D_MODEL), F32)],
        [pltpu.VMEM((ROW_TILE, D_FF), BF16)], cast)


def _tree_max(blocks):
    rows = [b[r:r + 8, c:c + LANES] for b in blocks
            for r in range(0, b.shape[0], 8) for c in range(0, b.shape[1], LANES)]
    while len(rows) > 1:
        rows = [jnp.maximum(a, b) for a, b in zip(rows[0::2], rows[1::2])] + rows[len(rows) & ~1:]
    return jnp.max(rows[0])


def _attn_kernel(q_ref, k_ref, v_ref, suf_ref, o_ref, carry_ref, acc_ref, live_ref):
    qb = pl.program_id(2)
    lane = lax.broadcasted_iota(jnp.int32, (ATT_TILE, LANES), 1)
    first_head = lane < HEAD_DIM
    row = lax.broadcasted_iota(jnp.int32, (ATT_TILE, 2 * ATT_TILE), 0)
    col = lax.broadcasted_iota(jnp.int32, (ATT_TILE, 2 * ATT_TILE), 1) & (ATT_TILE - 1)
    below_diag = col < row
    contract_last = (((1,), (1,)), ((), ()))

    def per_head(x):
        xz = jnp.zeros_like(x)
        return jnp.concatenate([jnp.where(first_head, x, xz), jnp.where(first_head, xz, x)], axis=0)

    def key_tile(kt):
        ks = pl.multiple_of(kt * ATT_TILE, ATT_TILE)
        return per_head(k_ref[pl.ds(ks, ATT_TILE), :]), per_head(v_ref[pl.ds(ks, ATT_TILE), :])

    def softplus(z):
        return jnp.maximum(z, 0.0) + jnp.log(1.0 + jnp.exp2(jnp.abs(z) * -LOG2E))

    def suffix_lhs(sp):
        return sp.astype(BF16)

    def row_totals(neg_sums):
        n = neg_sums.shape[0]
        return jnp.concatenate(
            [jnp.broadcast_to(neg_sums[:, 0:1], (n, ATT_TILE)),
             jnp.broadcast_to(neg_sums[:, ATT_TILE:ATT_TILE + 1], (n, ATT_TILE))], axis=1)

    def rows_used(i, j):
        return ATT_TILE if i - j < ATT_STATIC - 1 else ATT_TRIM

    def static_part(first_block):
        halo = 0 if first_block else ATT_STATIC - 1
        tiles = range(-halo, ATT_NSUB)
        users = {j: [i for i in range(ATT_NSUB) if 0 <= i - j < ATT_STATIC] for j in tiles}
        kv = {j: key_tile(qb * ATT_NSUB + j) for j in tiles}
        z = {}
        for j in tiles:
            r0 = users[j][0] * ATT_TILE
            n = sum(rows_used(i, j) for i in users[j])
            zz = lax.dot_general(q_ref[r0:r0 + n, :], kv[j][0], contract_last,
                                 preferred_element_type=F32)
            off = 0
            for i in users[j]:
                z[i, j] = zz[off:off + rows_used(i, j)]
                off += rows_used(i, j)
        acc = [None] * ATT_NSUB
        finals = []
        groups = [range(g, g + ATT_GROUP) for g in range(0, ATT_NSUB, ATT_GROUP)]

        def suffix_sums(group):
            pairs = [(i, j) for i in group for j in range(i, i - ATT_STATIC, -1) if j >= -halo]
            lhs, start, off = [], {}, 0
            for i, j in pairs:
                if i == j:
                    z[i, j] = jnp.where(below_diag, z[i, j], -MASKED_SCORE)
                lhs.append(suffix_lhs(softplus(z[i, j])))
                start[i, j] = off
                off += rows_used(i, j)
            return start, jnp.dot(jnp.concatenate(lhs, axis=0), suf_ref[...],
                                  preferred_element_type=F32)

        def weigh_values(group, start, neg_sums):
            g = group[0]
            wb = {}
            for i in group:
                carry = None
                for j in range(i, i - ATT_STATIC, -1):
                    if j < -halo:
                        continue
                    n = rows_used(i, j)
                    ns = neg_sums[start[i, j]:start[i, j] + n]
                    arg = z[i, j] + ns
                    if carry is not None:
                        arg = arg + carry[:n]
                    wb[i, j] = jnp.exp(arg).astype(BF16)
                    tot = row_totals(ns)
                    if carry is None:
                        carry = tot
                    elif n == ATT_TILE:
                        carry = carry + tot
                    else:
                        carry = jnp.concatenate([carry[:n] + tot, carry[n:]], axis=0)
                carry_ref[i] = carry
                finals.append(carry)
            for j in range(g + ATT_GROUP - 1, max(g - ATT_STATIC, -halo - 1), -1):
                mine = [i for i in group if (i, j) in wb]
                if not mine:
                    continue
                out = jnp.dot(jnp.concatenate([wb[i, j] for i in mine], axis=0), kv[j][1],
                              preferred_element_type=F32)
                off = 0
                for i in mine:
                    n = rows_used(i, j)
                    part = out[off:off + n]
                    off += n
                    if acc[i] is None:
                        acc[i] = part
                    elif n == ATT_TILE:
                        acc[i] = acc[i] + part
                    else:
                        acc[i] = jnp.concatenate([acc[i][:n] + part, acc[i][n:]], axis=0)
        pending = []
        for group in groups:
            pending.append((group,) + suffix_sums(group))
            if len(pending) > ATT_AHEAD:
                weigh_values(*pending.pop(0))
        for waiting in pending:
            weigh_values(*waiting)
        live_ref[0] = (_tree_max(finals) > EXP_ZERO_BELOW).astype(jnp.int32)
        for i in range(ATT_NSUB):
            acc_ref[i] = acc[i]
            o_ref[i * ATT_TILE:(i + 1) * ATT_TILE, :] = acc[i].astype(BF16)

    @pl.when(qb == 0)
    def _():
        static_part(True)

    @pl.when(qb > 0)
    def _():
        static_part(False)

    @pl.when(live_ref[0] > 0)
    def _():
        def finish(i, r0, r1, first_tile):
            rows = pl.ds(pl.multiple_of(i * ATT_TILE + r0, ATT_TRIM), r1 - r0)
            q = q_ref[rows, :]

            def cond(state):
                kt, live, _, _ = state
                return jnp.logical_and(kt >= 0, live)

            def body(state):
                kt, _, carry, acc = state
                kk, vv = key_tile(kt)
                z = lax.dot_general(q, kk, contract_last, preferred_element_type=F32)
                ns = jnp.dot(suffix_lhs(softplus(z)), suf_ref[...], preferred_element_type=F32)
                w = jnp.exp(z + ns + carry).astype(BF16)
                acc = acc + jnp.dot(w, vv, preferred_element_type=F32)
                carry = carry + row_totals(ns)
                return kt - 1, jnp.max(carry) > EXP_ZERO_BELOW, carry, acc

            carry = carry_ref[i, r0:r1, :]
            state = (first_tile, jnp.max(carry) > EXP_ZERO_BELOW, carry, acc_ref[i, r0:r1, :])
            acc = lax.while_loop(cond, body, state)[3]
            o_ref[rows, :] = acc.astype(BF16)

        def finish_sub_tile(i, _):
            newest_unseen = qb * ATT_NSUB + i - ATT_STATIC
            finish(i, 0, ATT_TRIM, newest_unseen)
            finish(i, ATT_TRIM, ATT_TILE, newest_unseen + 1)
            return 0

        lax.fori_loop(0, ATT_NSUB, finish_sub_tile, 0)


def _attention(q, k, v, suffix, batch, seq):
    t = q.shape[1]
    n_q = seq // ATT_BLOCK
    q_spec = pl.BlockSpec((None, ATT_BLOCK, LANES), lambda b, h, i: (h, b * n_q + i, 0))
    kv_spec = pl.BlockSpec((None, seq, LANES), lambda b, h, i: (h, b, 0))
    return pl.pallas_call(
        _attn_kernel,
        grid=(batch, N_HEAD_PAIRS, n_q),
        in_specs=[q_spec, kv_spec, kv_spec,
                  _resident((2 * ATT_TILE, 2 * ATT_TILE), lambda b, h, i: (0, 0))],
        out_specs=q_spec,
        out_shape=jax.ShapeDtypeStruct((N_HEAD_PAIRS, t, LANES), BF16),
        scratch_shapes=[pltpu.VMEM((ATT_NSUB, ATT_TILE, 2 * ATT_TILE), F32),
                        pltpu.VMEM((ATT_NSUB, ATT_TILE, LANES), F32),
                        pltpu.SMEM((1,), jnp.int32)],
        compiler_params=_params("arbitrary", "arbitrary", "arbitrary"),
        name="attn",
    )(q, k, v, suffix)


def _attention_constants():
    j = lax.broadcasted_iota(jnp.int32, (2 * ATT_TILE, 2 * ATT_TILE), 0)
    s = lax.broadcasted_iota(jnp.int32, (2 * ATT_TILE, 2 * ATT_TILE), 1)
    same_head = (j // ATT_TILE) == (s // ATT_TILE)
    suffix = jnp.where(jnp.logical_and(same_head, j >= s), -1.0, 0.0).astype(BF16)
    r = lax.broadcasted_iota(jnp.int32, (MXU_WIDTH, MXU_WIDTH), 0) // HEAD_DIM
    c = lax.broadcasted_iota(jnp.int32, (MXU_WIDTH, MXU_WIDTH), 1) // HEAD_DIM
    avg = jnp.where(r == c, 1.0 / HEAD_DIM, 0.0).astype(BF16)
    return suffix, avg


def kernel(x, p, norm_ffn1, w_ffn1_gu, w_ffn1_down, norm_mix, w_qkv, q_norm, k_norm, w_o,
           w_pool_in, w_pool_grp, pool_scale, norm_ffn2, w_ffn2_gu, w_ffn2_down, norm_ple,
           w_ple_gate, w_ple_proj):
    batch, seq, _ = x.shape
    depth = norm_ffn1.shape[0]
    t = batch * seq
    bf = lambda w: w.astype(BF16)
    row3 = lambda g: g.reshape(g.shape[0], 1, g.shape[1])
    assert t == N_ROW_STEPS * ROW_TILE
    n1, nm, n2, npl = row3(norm_ffn1), row3(norm_mix), row3(norm_ffn2), row3(norm_ple)
    qg = row3(jnp.tile(q_norm, (1, N_HEADS)))
    kg = row3(jnp.tile(k_norm, (1, N_HEADS)))
    psc = row3(pool_scale)
    suffix, avg = _attention_constants()
    p2 = p.reshape(depth, t, PLE_DIM)

    xs = x.reshape(t, D_MODEL)
    ffn_w = (bf(w_ffn1_gu[0]), bf(w_ffn1_down[0]))
    wqkv = bf(w_qkv[0])
    merged = [w.reshape(-1, w.shape[-1])
              for w in (w_o, w_pool_in, w_pool_grp, w_ple_gate, w_ple_proj)]
    small = [(w_qkv, 1, D_MODEL // N_ROW_STEPS, 1)] + [
        (w, None, w.shape[0] // N_ROW_STEPS, 1) for w in merged]
    for i in range(depth):
        j = i // 2
        casts = _ffn_casts(w_ffn2_gu, w_ffn2_down, i)
        if i == 0:
            x1, q, k, v, *ffn_w, wqkv_next, wo, wpin, wpgrp, wpg, wpp = _ffn_qkv(
                xs, n1, ffn_w, nm, wqkv, qg, kg, avg, i, j, casts + small)
            wo, wpin, wpgrp, wpg, wpp = (
                c.reshape(w.shape) for c, w in
                zip((wo, wpin, wpgrp, wpg, wpp), (w_o, w_pool_in, w_pool_grp, w_ple_gate, w_ple_proj)))
        elif i % 2 == 0:
            x1, q, k, v, *ffn_w = _ffn_qkv(xs, n1, ffn_w, nm, wqkv_next, qg, kg, avg, i, j, casts)
        else:
            x1, *ffn_w = _ffn_pool(xs, n1, ffn_w, nm, wpin, wpgrp, psc, batch, seq, i, j, casts)
        o = _attention(q, k, v, suffix, batch, seq) if i % 2 == 0 else None
        casts = _ffn_casts(w_ffn1_gu, w_ffn1_down, i + 1) if i + 1 < depth else []
        xs, *ffn_w = _ffn_ple(x1, o, wo, n2, ffn_w, p2, npl, wpg, wpp, i, j, casts)
    return xs.reshape(batch, seq, D_MODEL)
```

```python
import jax
import jax.numpy as jnp
from jax import lax
from jax.experimental import pallas as pl
from jax.experimental.pallas import tpu as pltpu

D_MODEL = 1024
N_HEADS = 16
HEAD_DIM = D_MODEL // N_HEADS
D_FF = 2816
PLE_DIM = 256
POOL_WINDOWS = (2, 4, 8, 16)
POOL_GROUP = D_MODEL // len(POOL_WINDOWS)
EPS = 1e-6

LANES = 128
HEADS_PER_BLOCK = LANES // HEAD_DIM
N_HEAD_PAIRS = N_HEADS // HEADS_PER_BLOCK
MXU_WIDTH = 256
FF_CHUNK = MXU_WIDTH
N_FF_CHUNKS = D_FF // FF_CHUNK
ROW_TILE = 512
N_ROW_STEPS = 32
CAST_GU_ROWS = D_MODEL // N_ROW_STEPS
CAST_DOWN_ROWS = 2 * D_FF // N_ROW_STEPS
ATT_TILE = 128
ATT_NSUB = 8
ATT_BLOCK = ATT_TILE * ATT_NSUB
ATT_STATIC = 3
ATT_TRIM = 32
ATT_GROUP = 2
ATT_AHEAD = 2
MASKED_SCORE = 1e30
LOG2E = 1.4426950408889634
POOL_HALO = 32
assert all(w == 2 << i for i, w in enumerate(POOL_WINDOWS)) and POOL_HALO == 8 * len(POOL_WINDOWS)
POOL_LEVEL_COLS = sum(range(len(POOL_WINDOWS))) * POOL_GROUP
EXP_ZERO_BELOW = -104.0
VMEM_LIMIT = 56 * 1024 * 1024

F32 = jnp.float32
BF16 = jnp.bfloat16


def _rms(x, g):
    ms = jnp.mean(x * x, axis=-1, keepdims=True)
    return x * lax.rsqrt(ms + EPS) * g


def _params(*sem):
    return pltpu.CompilerParams(dimension_semantics=sem, vmem_limit_bytes=VMEM_LIMIT)


def _resident(shape, index_map):
    return pl.BlockSpec(shape, index_map, pipeline_mode=pl.Buffered(1))


def _layer_block(shape, layer):
    zeros = (0,) * len(shape)
    return _resident((None,) + tuple(shape), lambda *_: (layer,) + zeros)


def _split_hi_lo(x):
    hi = lax.bitcast_convert_type(
        lax.bitcast_convert_type(x, jnp.uint32) & jnp.uint32(0xFFFF0000), F32)
    return hi.astype(BF16), (x - hi).astype(BF16)


def _ffn_stage(x, g_ref, wgu_ref, wd_ref, a_ref):
    h = _rms(x, g_ref[...]).astype(BF16)
    for c in range(N_FF_CHUNKS):
        lo = c * FF_CHUNK
        gate = jnp.dot(h, wgu_ref[:, lo:lo + FF_CHUNK], preferred_element_type=F32)
        up = jnp.dot(h, wgu_ref[:, D_FF + lo:D_FF + lo + FF_CHUNK], preferred_element_type=F32)
        a_ref[:, lo:lo + FF_CHUNK] = (gate * jax.nn.sigmoid(gate) * up).astype(BF16)
    return x + 0.5 * jnp.dot(a_ref[...], wd_ref[...], preferred_element_type=F32)


def _ple_stage(x, p_ref, g_ref, wg_ref, wp_ref):
    h = _rms(x, g_ref[...]).astype(BF16)
    gate = jax.nn.sigmoid(jnp.dot(h, wg_ref[...], preferred_element_type=F32))
    proj = jnp.dot(p_ref[...].astype(BF16), wp_ref[...], preferred_element_type=F32)
    return x + gate * proj


def _head_norm(y, gain, avg):
    width = avg.shape[0]
    cols = []
    for c in range(D_MODEL // width):
        blk = y[:, c * width:(c + 1) * width]
        ms = jnp.dot((blk * blk).astype(BF16), avg, preferred_element_type=F32)
        cols.append(blk * lax.rsqrt(ms + EPS))
    return jnp.concatenate(cols, axis=1) * gain


def _qkv_stage(x, g_ref, w_ref, qg_ref, kg_ref, avg_ref, q_ref, k_ref, v_ref):
    h = _rms(x, g_ref[...]).astype(BF16)
    avg = avg_ref[...]
    def put(ref, y):
        for hp in range(N_HEAD_PAIRS):
            ref[hp] = y[:, hp * LANES:(hp + 1) * LANES]

    q = jnp.dot(h, w_ref[:, 0:D_MODEL], preferred_element_type=F32)
    put(q_ref, (_head_norm(q, qg_ref[...], avg) * (HEAD_DIM ** -0.5)).astype(BF16))
    k = jnp.dot(h, w_ref[:, D_MODEL:2 * D_MODEL], preferred_element_type=F32)
    put(k_ref, _head_norm(k, kg_ref[...], avg).astype(BF16))
    v = jnp.dot(h, w_ref[:, 2 * D_MODEL:3 * D_MODEL], preferred_element_type=F32)
    put(v_ref, v.astype(BF16))


def _pool_stage(x, si, g_ref, win_ref, wgrp_ref, sc_ref, o_ref, hist_ref, lvl_ref):
    h = _rms(x, g_ref[...]).astype(BF16)
    u = jnp.dot(h, win_ref[...], preferred_element_type=F32)

    @pl.when(si == 0)
    def _():
        hist_ref[0:POOL_HALO, :] = jnp.zeros((POOL_HALO, D_MODEL), F32)

    @pl.when(si > 0)
    def _():
        hist_ref[0:POOL_HALO, :] = hist_ref[ROW_TILE:ROW_TILE + POOL_HALO, :]

    hist_ref[POOL_HALO:POOL_HALO + ROW_TILE, :] = u
    n_rows = POOL_HALO + ROW_TILE
    n_groups = len(POOL_WINDOWS)
    pos = si * ROW_TILE + lax.broadcasted_iota(jnp.int32, (ROW_TILE, 1), 0)

    def lvl_col(k, gi):
        return sum(n_groups - m for m in range(1, k)) * POOL_GROUP + (gi - k) * POOL_GROUP

    def shifted_sum(k, gi, r0):
        shift = 1 << (k - 1)
        if k == 1:
            src, c = hist_ref, gi * POOL_GROUP
        else:
            src, c = lvl_ref, lvl_col(k - 1, gi)
        return (src[r0:n_rows, c:c + POOL_GROUP]
                + src[r0 - shift:n_rows - shift, c:c + POOL_GROUP])

    for k in range(1, n_groups + 1):
        for gi in range(k, n_groups):
            c = lvl_col(k, gi)
            lvl_ref[8 * k:n_rows, c:c + POOL_GROUP] = shifted_sum(k, gi, 8 * k)
        gi, win = k - 1, POOL_WINDOWS[k - 1]
        lo = gi * POOL_GROUP
        wsum = shifted_sum(k, gi, POOL_HALO)
        cnt = jnp.minimum(pos + 1, win).astype(F32)
        pooled = wsum / cnt - u[:, lo:lo + POOL_GROUP]
        y = jnp.dot(pooled.astype(BF16), wgrp_ref[gi], preferred_element_type=F32)
        o_ref[:, lo:lo + POOL_GROUP] = x[:, lo:lo + POOL_GROUP] + y * sc_ref[:, lo:lo + POOL_GROUP]


def _with_weight_cast(body, n_in, n_out, n_cast):
    def kernel(*refs):
        ins, src = refs[:n_in], refs[n_in:n_in + n_cast]
        outs = refs[n_in + n_cast:n_in + n_cast + n_out]
        dst = refs[n_in + n_cast + n_out:n_in + 2 * n_cast + n_out]
        scratch = refs[n_in + 2 * n_cast + n_out:]
        body(*ins, *outs, *scratch)
        for s, d in zip(src, dst):
            d[...] = s[...].astype(BF16)
    return kernel


def _ffn_casts(w_gu, w_down, layer):
    return [(w_gu, layer, CAST_GU_ROWS, 1), (w_down, layer, CAST_DOWN_ROWS, 2)]


def _per_token_call(body, name, grid, step, in_specs, args, out_specs, out_shape, scratch, casts):
    in_specs, out_specs, out_shape = list(in_specs), list(out_specs), list(out_shape)
    if casts:
        body = _with_weight_cast(body, len(in_specs), len(out_specs), len(casts))
    for src, layer, rows, per in casts:
        total, width = src.shape[-2:]
        assert total * per == rows * N_ROW_STEPS
        if layer is None:
            in_specs.append(pl.BlockSpec(
                (rows, width), lambda *g, per=per: (step(*g) // per, 0)))
        else:
            in_specs.append(pl.BlockSpec(
                (None, rows, width), lambda *g, per=per, layer=layer: (layer, step(*g) // per, 0)))
    for src, layer, rows, per in casts:
        total, width = src.shape[-2:]
        out_specs.append(pl.BlockSpec((rows, width), lambda *g, per=per: (step(*g) // per, 0)))
        out_shape.append(jax.ShapeDtypeStruct((total, width), BF16))
    args = tuple(args) + tuple(c[0] for c in casts)
    return pl.pallas_call(
        body,
        grid=grid,
        in_specs=in_specs,
        out_specs=out_specs,
        out_shape=out_shape,
        scratch_shapes=scratch,
        compiler_params=_params(*(["arbitrary"] * len(grid))),
        name=name,
    )(*args)


def _ffn_weight_specs():
    return [_resident((D_MODEL, 2 * D_FF), lambda *_: (0, 0)),
            _resident((D_FF, D_MODEL), lambda *_: (0, 0))]


def _ffn_qkv_kernel(x_ref, g1_ref, wgu_ref, wd_ref, gm_ref, wqkv_ref, qg_ref, kg_ref, avg_ref,
                    x1_ref, q_ref, k_ref, v_ref, a_ref):
    x1 = _ffn_stage(x_ref[...], g1_ref, wgu_ref, wd_ref, a_ref)
    x1_ref[...] = x1
    _qkv_stage(x1, gm_ref, wqkv_ref, qg_ref, kg_ref, avg_ref, q_ref, k_ref, v_ref)


def _ffn_qkv(x, n1, ffn_w, nm, w_qkv, q_gain, k_gain, avg, layer, j, cast):
    t = x.shape[0]
    row = pl.BlockSpec((ROW_TILE, D_MODEL), lambda m: (m, 0))
    act = jax.ShapeDtypeStruct((N_HEAD_PAIRS, t, LANES), BF16)
    act_spec = pl.BlockSpec((N_HEAD_PAIRS, ROW_TILE, LANES), lambda m: (0, m, 0))
    return _per_token_call(
        _ffn_qkv_kernel, "ffn_qkv", (t // ROW_TILE,), lambda m: m,
        [row, _layer_block((1, D_MODEL), layer)] + _ffn_weight_specs() + [
            _layer_block((1, D_MODEL), layer),
            _resident((D_MODEL, 3 * D_MODEL), lambda *_: (0, 0)),
            _layer_block((1, D_MODEL), j),
            _layer_block((1, D_MODEL), j),
            _resident((MXU_WIDTH, MXU_WIDTH), lambda m: (0, 0)),
        ],
        (x, n1) + tuple(ffn_w) + (nm, w_qkv, q_gain, k_gain, avg),
        [row, act_spec, act_spec, act_spec],
        [jax.ShapeDtypeStruct((t, D_MODEL), F32), act, act, act],
        [pltpu.VMEM((ROW_TILE, D_FF), BF16)],
        cast)


def _ffn_pool_kernel(x_ref, g1_ref, wgu_ref, wd_ref, gm_ref, win_ref, wgrp_ref, sc_ref,
                     o_ref, a_ref, hist_ref, lvl_ref):
    x1 = _ffn_stage(x_ref[...], g1_ref, wgu_ref, wd_ref, a_ref)
    _pool_stage(x1, pl.program_id(1), gm_ref, win_ref, wgrp_ref, sc_ref, o_ref, hist_ref, lvl_ref)


def _ffn_pool(x, n1, ffn_w, nm, w_in, w_grp, scale, batch, seq, layer, j, cast):
    t = x.shape[0]
    n_s = seq // ROW_TILE
    row = pl.BlockSpec((ROW_TILE, D_MODEL), lambda b, s: (b * n_s + s, 0))
    n_g = len(POOL_WINDOWS)
    return _per_token_call(
        _ffn_pool_kernel, "ffn_pool", (batch, n_s), lambda b, s: b * n_s + s,
        [row, _layer_block((1, D_MODEL), layer)] + _ffn_weight_specs() + [
            _layer_block((1, D_MODEL), layer),
            _layer_block((D_MODEL, D_MODEL), j),
            _layer_block((n_g, POOL_GROUP, POOL_GROUP), j),
            _layer_block((1, D_MODEL), j),
        ],
        (x, n1) + tuple(ffn_w) + (nm, w_in, w_grp, scale),
        [row],
        [jax.ShapeDtypeStruct((t, D_MODEL), F32)],
        [pltpu.VMEM((ROW_TILE, D_FF), BF16),
         pltpu.VMEM((ROW_TILE + POOL_HALO, D_MODEL), F32),
         pltpu.VMEM((ROW_TILE + POOL_HALO, POOL_LEVEL_COLS), F32)],
        cast)


def _proj_ffn_ple_kernel(x_ref, o_ref, wo_ref, g2_ref, wgu_ref, wd_ref, p_ref, gp_ref, wg_ref,
                         wp_ref, y_ref, a_ref):
    o = jnp.concatenate([o_ref[hp] for hp in range(N_HEAD_PAIRS)], axis=1)
    x = x_ref[...] + jnp.dot(o, wo_ref[...], preferred_element_type=F32)
    x = _ffn_stage(x, g2_ref, wgu_ref, wd_ref, a_ref)
    y_ref[...] = _ple_stage(x, p_ref, gp_ref, wg_ref, wp_ref)


def _ffn_ple_kernel(x_ref, g2_ref, wgu_ref, wd_ref, p_ref, gp_ref, wg_ref, wp_ref, y_ref, a_ref):
    x = _ffn_stage(x_ref[...], g2_ref, wgu_ref, wd_ref, a_ref)
    y_ref[...] = _ple_stage(x, p_ref, gp_ref, wg_ref, wp_ref)


def _ffn_ple(x, o, w_o, n2, ffn_w, p, npl, w_gate, w_proj, layer, j, cast):
    t = x.shape[0]
    row = pl.BlockSpec((ROW_TILE, D_MODEL), lambda m: (m, 0))
    tail_specs = [_layer_block((1, D_MODEL), layer)] + _ffn_weight_specs() + [
        pl.BlockSpec((None, ROW_TILE, PLE_DIM), lambda m: (layer, m, 0)),
        _layer_block((1, D_MODEL), layer),
        _layer_block((D_MODEL, D_MODEL), layer),
        _layer_block((PLE_DIM, D_MODEL), layer),
    ]
    tail_args = (n2,) + tuple(ffn_w) + (p, npl, w_gate, w_proj)
    if o is None:
        body, specs, args, name = _ffn_ple_kernel, [row] + tail_specs, (x,) + tail_args, "ffn_ple"
    else:
        body, name = _proj_ffn_ple_kernel, "proj_ffn_ple"
        o_spec = pl.BlockSpec((N_HEAD_PAIRS, ROW_TILE, LANES), lambda m: (0, m, 0))
        specs = [row, o_spec, _layer_block((D_MODEL, D_MODEL), j)] + tail_specs
        args = (x, o, w_o) + tail_args
    return _per_token_call(
        body, name, (t // ROW_TILE,), lambda m: m, specs, args,
        [row], [jax.ShapeDtypeStruct((t, D_MODEL), F32)],
        [pltpu.VMEM((ROW_TILE, D_FF), BF16)], cast)


def _tree_max(blocks):
    rows = [b[r:r + 8, c:c + LANES] for b in blocks
            for r in range(0, b.shape[0], 8) for c in range(0, b.shape[1], LANES)]
    while len(rows) > 1:
        rows = [jnp.maximum(a, b) for a, b in zip(rows[0::2], rows[1::2])] + rows[len(rows) & ~1:]
    return jnp.max(rows[0])


def _attn_kernel(q_ref, k_ref, v_ref, suf_ref, o_ref, carry_ref, acc_ref, live_ref):
    qb = pl.program_id(2)
    lane = lax.broadcasted_iota(jnp.int32, (ATT_TILE, LANES), 1)
    first_head = lane < HEAD_DIM
    row = lax.broadcasted_iota(jnp.int32, (ATT_TILE, 2 * ATT_TILE), 0)
    col = lax.broadcasted_iota(jnp.int32, (ATT_TILE, 2 * ATT_TILE), 1) & (ATT_TILE - 1)
    below_diag = col < row
    contract_last = (((1,), (1,)), ((), ()))

    def per_head(x):
        xz = jnp.zeros_like(x)
        return jnp.concatenate([jnp.where(first_head, x, xz), jnp.where(first_head, xz, x)], axis=0)

    def key_tile(kt):
        ks = pl.multiple_of(kt * ATT_TILE, ATT_TILE)
        return per_head(k_ref[pl.ds(ks, ATT_TILE), :]), per_head(v_ref[pl.ds(ks, ATT_TILE), :])

    def softplus(z):
        return jnp.maximum(z, 0.0) + jnp.log(1.0 + jnp.exp2(jnp.abs(z) * -LOG2E))

    def suffix_lhs(sp):
        return sp.astype(BF16)

    def row_totals(neg_sums):
        n = neg_sums.shape[0]
        return jnp.concatenate(
            [jnp.broadcast_to(neg_sums[:, 0:1], (n, ATT_TILE)),
             jnp.broadcast_to(neg_sums[:, ATT_TILE:ATT_TILE + 1], (n, ATT_TILE))], axis=1)

    def rows_used(i, j):
        return ATT_TILE if i - j < ATT_STATIC - 1 else ATT_TRIM

    def static_part(first_block):
        halo = 0 if first_block else ATT_STATIC - 1
        tiles = range(-halo, ATT_NSUB)
        users = {j: [i for i in range(ATT_NSUB) if 0 <= i - j < ATT_STATIC] for j in tiles}
        kv = {j: key_tile(qb * ATT_NSUB + j) for j in tiles}
        z = {}
        acc = [None] * ATT_NSUB
        finals = []
        groups = [range(g, g + ATT_GROUP) for g in range(0, ATT_NSUB, ATT_GROUP)]

        def suffix_sums(group):
            for j in tiles:
                mine = [i for i in group if i in users[j]]
                if not mine:
                    continue
                r0 = mine[0] * ATT_TILE
                n = sum(rows_used(i, j) for i in mine)
                zz = lax.dot_general(q_ref[r0:r0 + n, :], kv[j][0], contract_last,
                                     preferred_element_type=F32)
                off = 0
                for i in mine:
                    z[i, j] = zz[off:off + rows_used(i, j)]
                    off += rows_used(i, j)
            pairs = [(i, j) for i in group for j in range(i, i - ATT_STATIC, -1) if j >= -halo]
            lhs, start, off = [], {}, 0
            for i, j in pairs:
                if i == j:
                    z[i, j] = jnp.where(below_diag, z[i, j], -MASKED_SCORE)
                lhs.append(suffix_lhs(softplus(z[i, j])))
                start[i, j] = off
                off += rows_used(i, j)
            return start, jnp.dot(jnp.concatenate(lhs, axis=0), suf_ref[...],
                                  preferred_element_type=F32)

        def weigh_values(group, start, neg_sums):
            g = group[0]
            wb = {}
            for i in group:
                carry = None
                for j in range(i, i - ATT_STATIC, -1):
                    if j < -halo:
                        continue
                    n = rows_used(i, j)
                    ns = neg_sums[start[i, j]:start[i, j] + n]
                    arg = z[i, j] + ns
                    if carry is not None:
                        arg = arg + carry[:n]
                    wb[i, j] = jnp.exp(arg).astype(BF16)
                    tot = row_totals(ns)
                    if carry is None:
                        carry = tot
                    elif n == ATT_TILE:
                        carry = carry + tot
                    else:
                        carry = jnp.concatenate([carry[:n] + tot, carry[n:]], axis=0)
                carry_ref[i] = carry
                finals.append(carry)
            for j in range(g + ATT_GROUP - 1, max(g - ATT_STATIC, -halo - 1), -1):
                mine = [i for i in group if (i, j) in wb]
                if not mine:
                    continue
                out = jnp.dot(jnp.concatenate([wb[i, j] for i in mine], axis=0), kv[j][1],
                              preferred_element_type=F32)
                off = 0
                for i in mine:
                    n = rows_used(i, j)
                    part = out[off:off + n]
                    off += n
                    if acc[i] is None:
                        acc[i] = part
                    elif n == ATT_TILE:
                        acc[i] = acc[i] + part
                    else:
                        acc[i] = jnp.concatenate([acc[i][:n] + part, acc[i][n:]], axis=0)
        pending = []
        for group in groups:
            pending.append((group,) + suffix_sums(group))
            if len(pending) > ATT_AHEAD:
                weigh_values(*pending.pop(0))
        for waiting in pending:
            weigh_values(*waiting)
        live_ref[0] = (_tree_max(finals) > EXP_ZERO_BELOW).astype(jnp.int32)
        for i in range(ATT_NSUB):
            acc_ref[i] = acc[i]
            o_ref[i * ATT_TILE:(i + 1) * ATT_TILE, :] = acc[i].astype(BF16)

    @pl.when(qb == 0)
    def _():
        static_part(True)

    @pl.when(qb > 0)
    def _():
        static_part(False)

    @pl.when(live_ref[0] > 0)
    def _():
        def finish(i, r0, r1, first_tile):
            rows = pl.ds(pl.multiple_of(i * ATT_TILE + r0, ATT_TRIM), r1 - r0)
            q = q_ref[rows, :]

            def cond(state):
                kt, live, _, _ = state
                return jnp.logical_and(kt >= 0, live)

            def body(state):
                kt, _, carry, acc = state
                kk, vv = key_tile(kt)
                z = lax.dot_general(q, kk, contract_last, preferred_element_type=F32)
                ns = jnp.dot(suffix_lhs(softplus(z)), suf_ref[...], preferred_element_type=F32)
                w = jnp.exp(z + ns + carry).astype(BF16)
                acc = acc + jnp.dot(w, vv, preferred_element_type=F32)
                carry = carry + row_totals(ns)
                return kt - 1, jnp.max(carry) > EXP_ZERO_BELOW, carry, acc

            carry = carry_ref[i, r0:r1, :]
            state = (first_tile, jnp.max(carry) > EXP_ZERO_BELOW, carry, acc_ref[i, r0:r1, :])
            acc = lax.while_loop(cond, body, state)[3]
            o_ref[rows, :] = acc.astype(BF16)

        def finish_sub_tile(i, _):
            newest_unseen = qb * ATT_NSUB + i - ATT_STATIC
            finish(i, 0, ATT_TRIM, newest_unseen)
            finish(i, ATT_TRIM, ATT_TILE, newest_unseen + 1)
            return 0

        lax.fori_loop(0, ATT_NSUB, finish_sub_tile, 0)


def _attention(q, k, v, suffix, batch, seq):
    t = q.shape[1]
    n_q = seq // ATT_BLOCK
    q_spec = pl.BlockSpec((None, ATT_BLOCK, LANES), lambda b, h, i: (h, b * n_q + i, 0))
    kv_spec = pl.BlockSpec((None, seq, LANES), lambda b, h, i: (h, b, 0))
    return pl.pallas_call(
        _attn_kernel,
        grid=(batch, N_HEAD_PAIRS, n_q),
        in_specs=[q_spec, kv_spec, kv_spec,
                  _resident((2 * ATT_TILE, 2 * ATT_TILE), lambda b, h, i: (0, 0))],
        out_specs=q_spec,
        out_shape=jax.ShapeDtypeStruct((N_HEAD_PAIRS, t, LANES), BF16),
        scratch_shapes=[pltpu.VMEM((ATT_NSUB, ATT_TILE, 2 * ATT_TILE), F32),
                        pltpu.VMEM((ATT_NSUB, ATT_TILE, LANES), F32),
                        pltpu.SMEM((1,), jnp.int32)],
        compiler_params=_params("arbitrary", "arbitrary", "arbitrary"),
        name="attn",
    )(q, k, v, suffix)


def _attention_constants():
    j = lax.broadcasted_iota(jnp.int32, (2 * ATT_TILE, 2 * ATT_TILE), 0)
    s = lax.broadcasted_iota(jnp.int32, (2 * ATT_TILE, 2 * ATT_TILE), 1)
    same_head = (j // ATT_TILE) == (s // ATT_TILE)
    suffix = jnp.where(jnp.logical_and(same_head, j >= s), -1.0, 0.0).astype(BF16)
    r = lax.broadcasted_iota(jnp.int32, (MXU_WIDTH, MXU_WIDTH), 0) // HEAD_DIM
    c = lax.broadcasted_iota(jnp.int32, (MXU_WIDTH, MXU_WIDTH), 1) // HEAD_DIM
    avg = jnp.where(r == c, 1.0 / HEAD_DIM, 0.0).astype(BF16)
    return suffix, avg


def kernel(x, p, norm_ffn1, w_ffn1_gu, w_ffn1_down, norm_mix, w_qkv, q_norm, k_norm, w_o,
           w_pool_in, w_pool_grp, pool_scale, norm_ffn2, w_ffn2_gu, w_ffn2_down, norm_ple,
           w_ple_gate, w_ple_proj):
    batch, seq, _ = x.shape
    depth = norm_ffn1.shape[0]
    t = batch * seq
    bf = lambda w: w.astype(BF16)
    row3 = lambda g: g.reshape(g.shape[0], 1, g.shape[1])
    assert t == N_ROW_STEPS * ROW_TILE
    n1, nm, n2, npl = row3(norm_ffn1), row3(norm_mix), row3(norm_ffn2), row3(norm_ple)
    qg = row3(jnp.tile(q_norm, (1, N_HEADS)))
    kg = row3(jnp.tile(k_norm, (1, N_HEADS)))
    psc = row3(pool_scale)
    suffix, avg = _attention_constants()
    p2 = p.reshape(depth, t, PLE_DIM)

    xs = x.reshape(t, D_MODEL)
    ffn_w = (bf(w_ffn1_gu[0]), bf(w_ffn1_down[0]))
    wqkv = bf(w_qkv[0])
    merged = [w.reshape(-1, w.shape[-1])
              for w in (w_o, w_pool_in, w_pool_grp, w_ple_gate, w_ple_proj)]
    small = [(w_qkv, 1, D_MODEL // N_ROW_STEPS, 1)] + [
        (w, None, w.shape[0] // N_ROW_STEPS, 1) for w in merged]
    for i in range(depth):
        j = i // 2
        casts = _ffn_casts(w_ffn2_gu, w_ffn2_down, i)
        if i == 0:
            x1, q, k, v, *ffn_w, wqkv_next, wo, wpin, wpgrp, wpg, wpp = _ffn_qkv(
                xs, n1, ffn_w, nm, wqkv, qg, kg, avg, i, j, casts + small)
            wo, wpin, wpgrp, wpg, wpp = (
                c.reshape(w.shape) for c, w in
                zip((wo, wpin, wpgrp, wpg, wpp), (w_o, w_pool_in, w_pool_grp, w_ple_gate, w_ple_proj)))
        elif i % 2 == 0:
            x1, q, k, v, *ffn_w = _ffn_qkv(xs, n1, ffn_w, nm, wqkv_next, qg, kg, avg, i, j, casts)
        else:
            x1, *ffn_w = _ffn_pool(xs, n1, ffn_w, nm, wpin, wpgrp, psc, batch, seq, i, j, casts)
        o = _attention(q, k, v, suffix, batch, seq) if i % 2 == 0 else None
        casts = _ffn_casts(w_ffn1_gu, w_ffn1_down, i + 1) if i + 1 < depth else []
        xs, *ffn_w = _ffn_ple(x1, o, wo, n2, ffn_w, p2, npl, wpg, wpp, i, j, casts)
    return xs.reshape(batch, seq, D_MODEL)
```
